```python
import math
import jax, jax.numpy as jnp
from jax import lax
import numpy as np

D_MODEL = 2048
BATCH = 4
SEQ = 4096
DEPTH = 4

GRID_W = 64
CTX_LEN = 256
N_MIXERS = 3
HEAD_DIM = 128
N_HEADS = D_MODEL // HEAD_DIM
N_KV_HEADS = N_HEADS // 4
REP = N_HEADS // N_KV_HEADS
QKV_DIM = (N_HEADS + 2 * N_KV_HEADS) * HEAD_DIM
WINDOW = 128
BLOCK = 128
ROPE_BASE = 10000.0
S5_GROUP = 16
S5_GROUPS = D_MODEL // S5_GROUP
S5_STATE = 64
S5_CHUNK = 128
N_GROUPS = 4
EXPERTS_PER_GROUP = 8
N_EXPERTS = N_GROUPS * EXPERTS_PER_GROUP
TOP_K = 2
D_EXPERT = D_MODEL // 4
MOE_BLOCK = 128
N_MOD = 6
EPS = 1e-6
N_S5 = (DEPTH + 2) // 3
N_SWA = (DEPTH + 1) // 3
N_GA = DEPTH // 3

kernel_name = 'hybrid_s5_swa_axial_moe_trunk'


def rmsnorm(x, g):
    xf = x.astype(jnp.float32)
    y = xf * lax.rsqrt(jnp.mean(xf * xf, axis=-1, keepdims=True) + EPS)
    return y.astype(x.dtype) * g


def modulate(u, shift, scale):
    return u * (1 + scale) + shift


def ada_mod(cond, w, b):
    m = cond @ w + b
    return [t[:, None, :] for t in jnp.split(m, N_MOD, axis=-1)]


def axial_rope_tables(rows, dtype):
    t_row = jnp.repeat(jnp.arange(rows), GRID_W).astype(jnp.float32)
    t_col = jnp.tile(jnp.arange(GRID_W), rows).astype(jnp.float32)
    half = HEAD_DIM // 2
    inv = ROPE_BASE ** (-jnp.arange(0, half, 2, dtype=jnp.float32) / half)
    ar = t_row[:, None] * inv
    ac = t_col[:, None] * inv
    ang = jnp.concatenate([ar, ar, ac, ac], axis=-1)
    return jnp.cos(ang).astype(dtype), jnp.sin(ang).astype(dtype)


def rope_2d(x, cos, sin):
    xs = x.reshape(x.shape[:-1] + (2, 2, HEAD_DIM // 4))
    rot = jnp.stack([-xs[..., 1, :], xs[..., 0, :]], axis=-2).reshape(x.shape)
    return x * cos[None, :, None, :] + rot * sin[None, :, None, :]


def project_qkv(h, w_qkv, q_g, k_g):
    b, n, _ = h.shape
    p = h @ w_qkv
    q, k, v = jnp.split(p, [N_HEADS * HEAD_DIM, (N_HEADS + N_KV_HEADS) * HEAD_DIM], axis=-1)
    q = rmsnorm(q.reshape(b, n, N_HEADS, HEAD_DIM), q_g)
    k = rmsnorm(k.reshape(b, n, N_KV_HEADS, HEAD_DIM), k_g)
    v = v.reshape(b, n, N_KV_HEADS, HEAD_DIM)
    return q, k, v


def gqa_softmax(q, k, v, mask=None, sink=None):
    logits = jnp.einsum('bqgrd,bkgd->bgrqk', q, k).astype(jnp.float32) * (HEAD_DIM ** -0.5)
    if mask is not None:
        logits = jnp.where(mask, logits, -jnp.inf)
    if sink is not None:
        s = jnp.broadcast_to(sink.astype(jnp.float32)[None, :, :, None, None], logits.shape[:-1] + (1,))
        logits = jnp.concatenate([logits, s], axis=-1)
    p = jax.nn.softmax(logits, axis=-1)
    if sink is not None:
        p = p[..., :-1]
    return jnp.einsum('bgrqk,bkgd->bqgrd', p.astype(v.dtype), v)


def window_sink_attention(u, uc, w_qkv, q_g, k_g, sink, w_o, cos, sin, with_ctx_out):
    b, s, _ = u.shape
    l = uc.shape[1]
    nb = s // BLOCK
    q, k, v = project_qkv(u, w_qkv, q_g, k_g)
    qc, kc, vc = project_qkv(uc, w_qkv, q_g, k_g)
    q = rope_2d(q, cos, sin)
    k = rope_2d(k, cos, sin)
    qb = q.reshape(b, nb, BLOCK, N_KV_HEADS, REP, HEAD_DIM).transpose(1, 0, 2, 3, 4, 5)
    pad = ((0, 0), (BLOCK, BLOCK), (0, 0), (0, 0))
    kp = jnp.pad(k, pad)
    vp = jnp.pad(v, pad)
    sink_gr = sink.reshape(N_KV_HEADS, REP)
    ctx_ok = jnp.ones((BLOCK, l), bool)

    def block(args):
        i, qi = args
        start = i * BLOCK
        kw = lax.dynamic_slice_in_dim(kp, start, 3 * BLOCK, axis=1)
        vw = lax.dynamic_slice_in_dim(vp, start, 3 * BLOCK, axis=1)
        qpos = start + jnp.arange(BLOCK)
        kpos = start - BLOCK + jnp.arange(3 * BLOCK)
        local = (kpos[None, :] >= 0) & (kpos[None, :] < s) & (jnp.abs(qpos[:, None] - kpos[None, :]) <= WINDOW)
        mask = jnp.concatenate([local, ctx_ok], axis=-1)
        return gqa_softmax(qi, jnp.concatenate([kw, kc], 1), jnp.concatenate([vw, vc], 1), mask, sink_gr)

    o = lax.map(block, (jnp.arange(nb), qb))
    o = o.transpose(1, 0, 2, 3, 4, 5).reshape(b, s, N_HEADS * HEAD_DIM) @ w_o
    oc = None
    if with_ctx_out:
        qc = qc.reshape(b, l, N_KV_HEADS, REP, HEAD_DIM)
        oc = gqa_softmax(qc, kc, vc, None, sink_gr).reshape(b, l, N_HEADS * HEAD_DIM) @ w_o
    return o, oc


def axial_attention(u, uc, w_qkv, q_g, k_g, w_o, cos, sin, with_ctx_out):
    b, s, _ = u.shape
    l = uc.shape[1]
    nb = s // BLOCK
    q, k, v = project_qkv(u, w_qkv, q_g, k_g)
    qc, kc, vc = project_qkv(uc, w_qkv, q_g, k_g)
    q = rope_2d(q, cos, sin)
    k = rope_2d(k, cos, sin)
    k_all = jnp.concatenate([k, kc], axis=1)
    v_all = jnp.concatenate([v, vc], axis=1)
    qb = q.reshape(b, nb, BLOCK, N_KV_HEADS, REP, HEAD_DIM).transpose(1, 0, 2, 3, 4, 5)
    o = lax.map(lambda qi: gqa_softmax(qi, k_all, v_all), qb)
    o = o.transpose(1, 0, 2, 3, 4, 5).reshape(b, s, N_HEADS * HEAD_DIM) @ w_o
    oc = None
    if with_ctx_out:
        qc = qc.reshape(b, l, N_KV_HEADS, REP, HEAD_DIM)
        oc = gqa_softmax(qc, kc, vc).reshape(b, l, N_HEADS * HEAD_DIM) @ w_o
    return o, oc


def s5_discretize(lam_re, lam_im, log_dt, b_re, b_im):
    lam = lax.complex(lam_re.astype(jnp.float32), lam_im.astype(jnp.float32))
    dt = jnp.exp(log_dt.astype(jnp.float32))[:, None]
    lam_bar = jnp.exp(lam * dt)
    bm = lax.complex(b_re.astype(jnp.float32), b_im.astype(jnp.float32))
    b_bar = ((lam_bar - 1) / lam)[..., None] * bm
    return lam_bar, b_bar


def _ssm_combine(left, right):
    a1, b1 = left
    a2, b2 = right
    return a1 * a2, a2 * b1 + b2


def s5_scan(u, lam_bar, b_bar, cm, h0):
    b, n = u.shape[:2]
    nc = n // S5_CHUNK
    uc = u.reshape(b, nc, S5_CHUNK, S5_GROUPS, S5_GROUP).transpose(1, 0, 2, 3, 4)

    def chunk_step(h, u_blk):
        bu = jnp.einsum('gpc,blgc->blgp', b_bar, u_blk)
        bu = bu.at[:, 0].add(lam_bar * h)
        a = jnp.broadcast_to(lam_bar, bu.shape)
        _, hs = lax.associative_scan(_ssm_combine, (a, bu), axis=1)
        y = jnp.einsum('gcp,blgp->blgc', cm, hs).real
        return hs[:, -1], y

    h_fin, ys = lax.scan(chunk_step, h0, uc)
    return ys.transpose(1, 0, 2, 3, 4).reshape(b, n, S5_GROUPS, S5_GROUP), h_fin


def s5_mixer(u, uc, lam_re, lam_im, log_dt, b_re, b_im, c_re, c_im, d, w_glu, b_glu, with_ctx_out):
    b, s, dm = u.shape
    l = uc.shape[1]
    ul = u.astype(jnp.float32).reshape(b, s, S5_GROUPS, S5_GROUP)
    ucf = uc.astype(jnp.float32).reshape(b, l, S5_GROUPS, S5_GROUP)
    y_lat = jnp.zeros_like(ul)
    y_ctx = jnp.zeros_like(ucf)
    for direction in range(2):
        flip = (lambda t: t[:, ::-1]) if direction == 1 else (lambda t: t)
        lam_bar, b_bar = s5_discretize(lam_re[direction], lam_im[direction], log_dt[direction],
                                       b_re[direction], b_im[direction])
        cm = lax.complex(c_re[direction].astype(jnp.float32), c_im[direction].astype(jnp.float32))
        h0 = jnp.zeros((b, S5_GROUPS, S5_STATE), jnp.complex64)
        yc, h_ctx = s5_scan(flip(ucf), lam_bar, b_bar, cm, h0)
        yl, _ = s5_scan(flip(ul), lam_bar, b_bar, cm, h_ctx)
        y_lat = y_lat + flip(yl)
        if with_ctx_out:
            y_ctx = y_ctx + flip(yc)
    dg = d.astype(jnp.float32).reshape(S5_GROUPS, S5_GROUP)

    def glu_out(y, uu, n):
        z = jax.nn.gelu((y + dg * uu).reshape(b, n, dm)).astype(u.dtype)
        return z * jax.nn.sigmoid(z @ w_glu + b_glu)

    o = glu_out(y_lat, ul, s)
    oc = glu_out(y_ctx, ucf, l) if with_ctx_out else None
    return o, oc


def moe_dispatch(h, expert, weight, w1, w3, w2):
    t, dm = h.shape
    n = expert.shape[0]
    tok = jnp.arange(n) // TOP_K
    order = jnp.argsort(expert)
    e_s = expert[order]
    tok_s = tok[order]
    w_s = weight[order]
    counts = jnp.bincount(expert, length=N_EXPERTS)
    padded = ((counts + MOE_BLOCK - 1) // MOE_BLOCK) * MOE_BLOCK
    pend = jnp.cumsum(padded)
    pstart = pend - padded
    sstart = jnp.cumsum(counts) - counts
    slot = pstart[e_s] + jnp.arange(n) - sstart[e_s]
    n_blocks = -(-n // MOE_BLOCK) + N_EXPERTS
    cap = n_blocks * MOE_BLOCK
    slot_tok = jnp.full((cap,), t, jnp.int32).at[slot].set(tok_s.astype(jnp.int32))
    slot_w = jnp.zeros((cap,), h.dtype).at[slot].set(w_s.astype(h.dtype))
    block_e = jnp.minimum(jnp.searchsorted(pend, jnp.arange(n_blocks) * MOE_BLOCK, side='right'), N_EXPERTS - 1)
    h_pad = jnp.concatenate([h, jnp.zeros((1, dm), h.dtype)], axis=0)

    def block(args):
        ei, toks = args
        xb = h_pad[toks]
        return (jax.nn.silu(xb @ w1[ei]) * (xb @ w3[ei])) @ w2[ei]

    yb = lax.map(block, (block_e, slot_tok.reshape(n_blocks, MOE_BLOCK)))
    y = yb.reshape(cap, dm) * slot_w[:, None]
    return jnp.zeros((t + 1, dm), h.dtype).at[slot_tok].add(y)[:t]


def hier_moe(h, w_grp, b_grp, w_rt, b_rt, w1, w3, w2):
    t = h.shape[0]
    g_logits = (h @ w_grp).astype(jnp.float32) + b_grp.astype(jnp.float32)
    g_prob = jax.nn.softmax(g_logits, axis=-1)
    g_idx = jnp.argmax(g_logits, axis=-1)
    g_w = jnp.take_along_axis(g_prob, g_idx[:, None], axis=-1)
    e_logits = ((h @ w_rt).astype(jnp.float32) + b_rt.astype(jnp.float32)).reshape(t, N_GROUPS, EXPERTS_PER_GROUP)
    e_logits = e_logits[jnp.arange(t), g_idx]
    e_prob = jax.nn.softmax(e_logits, axis=-1)
    top_p, top_i = lax.top_k(e_prob, TOP_K)
    gate = g_w * top_p / jnp.sum(top_p, axis=-1, keepdims=True)
    expert = g_idx[:, None] * EXPERTS_PER_GROUP + top_i
    return moe_dispatch(h, expert.reshape(-1), gate.reshape(-1), w1, w3, w2)


def setup_inputs(seed: int = 0) -> dict:
    key = jax.random.key(seed)
    keys = list(jax.random.split(key, 40))
    f32 = jnp.float32

    def nrm(shape, s):
        return jax.random.normal(keys.pop(), shape, f32) * s

    d = D_MODEL
    g, p, cg = S5_GROUPS, S5_STATE, S5_GROUP
    inp = {}
    inp['x'] = nrm((BATCH, SEQ, d), 1.0)
    inp['c'] = nrm((BATCH, d), 1.0)
    inp['ctx'] = nrm((BATCH, CTX_LEN, d), 1.0)
    inp['c_ctx'] = nrm((d,), 1.0)
    inp['w_mod'] = nrm((DEPTH, d, N_MOD * d), 0.2 * d ** -0.5)
    inp['b_mod'] = nrm((DEPTH, N_MOD * d), 0.02)
    inp['norm1_g'] = 1.0 + nrm((DEPTH, d), 0.02)
    inp['norm2_g'] = 1.0 + nrm((DEPTH, d), 0.02)
    inp['s5_lam_re'] = -0.5 + nrm((N_S5, 2, g, p), 0.01)
    inp['s5_lam_im'] = jnp.pi * jnp.arange(p, dtype=f32) + nrm((N_S5, 2, g, p), 0.01)
    inp['s5_log_dt'] = jax.random.uniform(keys.pop(), (N_S5, 2, g), f32, math.log(1e-3), math.log(1e-1))
    inp['s5_b_re'] = nrm((N_S5, 2, g, p, cg), (2 * cg) ** -0.5)
    inp['s5_b_im'] = nrm((N_S5, 2, g, p, cg), (2 * cg) ** -0.5)
    inp['s5_c_re'] = nrm((N_S5, 2, g, cg, p), p ** -0.5)
    inp['s5_c_im'] = nrm((N_S5, 2, g, cg, p), p ** -0.5)
    inp['s5_d'] = nrm((N_S5, d), 1.0)
    inp['s5_w_glu'] = nrm((N_S5, d, d), d ** -0.5)
    inp['s5_b_glu'] = nrm((N_S5, d), 0.02)
    inp['swa_w_qkv'] = nrm((N_SWA, d, QKV_DIM), d ** -0.5)
    inp['swa_q_g'] = 1.0 + nrm((N_SWA, HEAD_DIM), 0.02)
    inp['swa_k_g'] = 1.0 + nrm((N_SWA, HEAD_DIM), 0.02)
    inp['swa_sink'] = nrm((N_SWA, N_HEADS), 1.0)
    inp['swa_w_o'] = nrm((N_SWA, N_HEADS * HEAD_DIM, d), (N_HEADS * HEAD_DIM) ** -0.5)
    inp['ga_w_qkv'] = nrm((N_GA, d, QKV_DIM), d ** -0.5)
    inp['ga_q_g'] = 1.0 + nrm((N_GA, HEAD_DIM), 0.02)
    inp['ga_k_g'] = 1.0 + nrm((N_GA, HEAD_DIM), 0.02)
    inp['ga_w_o'] = nrm((N_GA, N_HEADS * HEAD_DIM, d), (N_HEADS * HEAD_DIM) ** -0.5)
    inp['moe_w_grp'] = nrm((DEPTH, d, N_GROUPS), d ** -0.5)
    inp['moe_b_grp'] = nrm((DEPTH, N_GROUPS), 0.01)
    inp['moe_w_rt'] = nrm((DEPTH, d, N_EXPERTS), d ** -0.5)
    inp['moe_b_rt'] = nrm((DEPTH, N_EXPERTS), 0.01)
    inp['moe_w1'] = nrm((DEPTH, N_EXPERTS, d, D_EXPERT), d ** -0.5)
    inp['moe_w3'] = nrm((DEPTH, N_EXPERTS, d, D_EXPERT), d ** -0.5)
    inp['moe_w2'] = nrm((DEPTH, N_EXPERTS, D_EXPERT, d), D_EXPERT ** -0.5)
    return inp


def reference(x, c, ctx, c_ctx, w_mod, b_mod, norm1_g, norm2_g,
              s5_lam_re, s5_lam_im, s5_log_dt, s5_b_re, s5_b_im, s5_c_re, s5_c_im, s5_d, s5_w_glu, s5_b_glu,
              swa_w_qkv, swa_q_g, swa_k_g, swa_sink, swa_w_o,
              ga_w_qkv, ga_q_g, ga_k_g, ga_w_o,
              moe_w_grp, moe_b_grp, moe_w_rt, moe_b_rt, moe_w1, moe_w3, moe_w2):
    b, s, dm = x.shape
    l = ctx.shape[1]
    rows = s // GRID_W
    cos, sin = axial_rope_tables(rows, x.dtype)
    cond_lat = jax.nn.silu(c)
    cond_ctx = jax.nn.silu(c_ctx)[None, :]
    h, hc = x, ctx
    for i in range(DEPTH):
        last = i == DEPTH - 1
        kind, j = i % N_MIXERS, i // N_MIXERS
        sh1, sc1, g1, sh2, sc2, g2 = ada_mod(cond_lat, w_mod[i], b_mod[i])
        csh1, csc1, cg1, csh2, csc2, cg2 = ada_mod(cond_ctx, w_mod[i], b_mod[i])
        u = modulate(rmsnorm(h, norm1_g[i]), sh1, sc1)
        uc = modulate(rmsnorm(hc, norm1_g[i]), csh1, csc1)
        if kind == 0:
            o, oc = s5_mixer(u, uc, s5_lam_re[j], s5_lam_im[j], s5_log_dt[j], s5_b_re[j], s5_b_im[j],
                             s5_c_re[j], s5_c_im[j], s5_d[j], s5_w_glu[j], s5_b_glu[j], not last)
        elif kind == 1:
            o, oc = window_sink_attention(u, uc, swa_w_qkv[j], swa_q_g[j], swa_k_g[j], swa_sink[j],
                                          swa_w_o[j], cos, sin, not last)
        else:
            o, oc = axial_attention(u, uc, ga_w_qkv[j], ga_q_g[j], ga_k_g[j], ga_w_o[j], cos, sin, not last)
        h = h + g1 * o
        v = modulate(rmsnorm(h, norm2_g[i]), sh2, sc2).reshape(b * s, dm)
        moe_args = (moe_w_grp[i], moe_b_grp[i], moe_w_rt[i], moe_b_rt[i], moe_w1[i], moe_w3[i], moe_w2[i])
        if last:
            y = hier_moe(v, *moe_args)
            h = h + g2 * y.reshape(b, s, dm)
        else:
            hc = hc + cg1 * oc
            vc = modulate(rmsnorm(hc, norm2_g[i]), csh2, csc2).reshape(b * l, dm)
            y = hier_moe(jnp.concatenate([v, vc], axis=0), *moe_args)
            h = h + g2 * y[:b * s].reshape(b, s, dm)
            hc = hc + cg2 * y[b * s:].reshape(b, l, dm)
    return h
```

```python
import functools
import math

import jax
import jax.numpy as jnp
from jax import lax
from jax.experimental import pallas as pl
from jax.experimental.pallas import tpu as pltpu

F32 = jnp.float32
BF16 = jnp.bfloat16

EPS = 1e-6
N_MOD = 6
HEAD_DIM = 128
N_HEADS = 16
N_KV_HEADS = 4
REP = N_HEADS // N_KV_HEADS
WINDOW = 128
GRID_W = 64
ROPE_BASE = 10000.0
S5_GROUP = 16
S5_STATE = 64
N_GROUPS = 4
EXPERTS_PER_GROUP = 8
N_EXPERTS = N_GROUPS * EXPERTS_PER_GROUP
TOP_K = 2
ATTN_SCALE = HEAD_DIM ** -0.5

LANES = 128
TM = 256
S5_CHUNK = 16
MOE_TILE = 256
VMEM_LIMIT = 56 * 1024 * 1024


def _cparams(sem, vmem=VMEM_LIMIT):
    return pltpu.CompilerParams(dimension_semantics=sem, vmem_limit_bytes=vmem)


def _sigmoid(x):
    return 1.0 / (1.0 + jnp.exp(-x))


def _norm_mod(x, g, shift, scale):
    ms = jnp.mean(x * x, axis=-1, keepdims=True)
    y = x * lax.rsqrt(ms + EPS) * g
    return y * (1.0 + scale) + shift


def _mod_row(mod_ref, k):
    return mod_ref[0, k:k + 1, :]


def _mod_kernel(c_ref, w_ref, b_ref, o_ref):
    c = c_ref[...]
    cond = c * _sigmoid(c)
    acc = jnp.dot(cond.astype(BF16), w_ref[...].astype(BF16), preferred_element_type=F32)
    o_ref[...] = acc + b_ref[...]


def _ada_mod_all(c8, w_mod, b_mod):
    depth, d, n6 = w_mod.shape
    tn = 1536
    return pl.pallas_call(
        _mod_kernel,
        grid=(depth, n6 // tn),
        in_specs=[
            pl.BlockSpec((8, d), lambda l, j: (0, 0)),
            pl.BlockSpec((None, d, tn), lambda l, j: (l, 0, j)),
            pl.BlockSpec((None, 1, tn), lambda l, j: (l, 0, j)),
        ],
        out_specs=pl.BlockSpec((None, 8, tn), lambda l, j: (l, 0, j)),
        out_shape=jax.ShapeDtypeStruct((depth, 8, n6), F32),
        compiler_params=_cparams(("parallel", "parallel")),
        name="ada_mod",
    )(c8, w_mod, b_mod.reshape(depth, 1, n6))


def _row_spec(width, tpb):
    del tpb
    return pl.BlockSpec((TM, width), lambda i: (i, 0))


def _mod_spec(d, tpb, nb):
    return pl.BlockSpec((1, N_MOD, d), lambda i: (jnp.where(i % tpb == tpb - 1, nb, i // tpb), 0, 0))


def _const_spec(shape):
    nd = len(shape)
    return pl.BlockSpec(shape, lambda i: (0,) * nd)


def _u_kernel(h_ref, g_ref, mod_ref, o_ref):
    u = _norm_mod(h_ref[...], g_ref[...], _mod_row(mod_ref, 0), _mod_row(mod_ref, 1))
    o_ref[...] = u.astype(o_ref.dtype)


def _norm_mod_call(h, g, mod, tpb, nb):
    n, d = h.shape
    return pl.pallas_call(
        _u_kernel,
        grid=(n // TM,),
        in_specs=[_row_spec(d, tpb), _const_spec((1, d)), _mod_spec(d, tpb, nb)],
        out_specs=_row_spec(d, tpb),
        out_shape=jax.ShapeDtypeStruct((n, d), BF16),
        compiler_params=_cparams(("parallel",)),
        name="norm_mod",
    )(h, g.reshape(1, d), mod)


def _qkv_kernel(h_ref, g_ref, mod_ref, cos_ref, sin_ref, qg_ref, kg_ref, w_ref, q_ref, k_ref, v_ref):
    u = _norm_mod(h_ref[...], g_ref[...], _mod_row(mod_ref, 0), _mod_row(mod_ref, 1))
    p = jnp.dot(u.astype(BF16), w_ref[...], preferred_element_type=F32)
    cos = cos_ref[...]
    sin = sin_ref[...]
    lane = lax.broadcasted_iota(jnp.int32, (TM, HEAD_DIM), 1)
    first = (lane & 32) == 0

    def head(xh, gain, scale):
        ms = jnp.mean(xh * xh, axis=-1, keepdims=True)
        xn = xh * lax.rsqrt(ms + EPS) * gain
        rot = jnp.where(first, pltpu.roll(xn, HEAD_DIM - 32, 1), pltpu.roll(xn, 32, 1))
        return (xn * cos + rot * sin) * scale

    qg = qg_ref[...]
    kg = kg_ref[...]
    nq = N_HEADS * HEAD_DIM
    nk = N_KV_HEADS * HEAD_DIM
    for hh in range(N_HEADS):
        sl = slice(hh * HEAD_DIM, (hh + 1) * HEAD_DIM)
        q_ref[:, sl] = head(p[:, sl], qg, ATTN_SCALE).astype(BF16)
    for hh in range(N_KV_HEADS):
        sl = slice(hh * HEAD_DIM, (hh + 1) * HEAD_DIM)
        k_ref[:, sl] = head(p[:, nq + hh * HEAD_DIM:nq + (hh + 1) * HEAD_DIM], kg, 1.0).astype(BF16)
    v_ref[...] = p[:, nq + nk:].astype(BF16)


def _qkv_call(h, g, mod, cos, sin, q_g, k_g, w_qkv_b, tpb, nb):
    n, d = h.shape
    nq = N_HEADS * HEAD_DIM
    nk = N_KV_HEADS * HEAD_DIM
    return pl.pallas_call(
        _qkv_kernel,
        grid=(n // TM,),
        in_specs=[
            _row_spec(d, tpb), _const_spec((1, d)), _mod_spec(d, tpb, nb),
            pl.BlockSpec((TM, HEAD_DIM), lambda i: (i % tpb, 0)),
            pl.BlockSpec((TM, HEAD_DIM), lambda i: (i % tpb, 0)),
            _const_spec((1, HEAD_DIM)), _const_spec((1, HEAD_DIM)),
            _const_spec((d, nq + 2 * nk)),
        ],
        out_specs=[_row_spec(nq, tpb), _row_spec(nk, tpb), _row_spec(nk, tpb)],
        out_shape=[jax.ShapeDtypeStruct((n, nq), BF16), jax.ShapeDtypeStruct((n, nk), BF16),
                   jax.ShapeDtypeStruct((n, nk), BF16)],
        compiler_params=_cparams(("parallel",)),
        name="qkv_proj",
    )(h, g.reshape(1, d), mod, cos, sin, q_g.reshape(1, HEAD_DIM), k_g.reshape(1, HEAD_DIM), w_qkv_b)


def _oproj_kernel(a_ref, w_ref, h_ref, mod_ref, o_ref):
    acc = jnp.dot(a_ref[...], w_ref[...], preferred_element_type=F32)
    o_ref[...] = h_ref[...] + _mod_row(mod_ref, 2) * acc


def _oproj_call(a, w_o_b, h, mod, tpb, nb):
    n, d = h.shape
    return pl.pallas_call(
        _oproj_kernel,
        grid=(n // TM,),
        in_specs=[_row_spec(a.shape[1], tpb), _const_spec(w_o_b.shape), _row_spec(d, tpb), _mod_spec(d, tpb, nb)],
        out_specs=_row_spec(d, tpb),
        out_shape=jax.ShapeDtypeStruct((n, d), F32),
        compiler_params=_cparams(("parallel",)),
        name="o_proj",
    )(a, w_o_b, h, mod)


def _stack_heads(q_ref):
    return jnp.concatenate([q_ref[:, r * HEAD_DIM:(r + 1) * HEAD_DIM] for r in range(REP)], axis=0)


def _ga_kernel(q_ref, k_ref, v_ref, o_ref, m_scr, l_scr, acc_scr, *, n_kt):
    qt = pl.program_id(2)
    kt = pl.program_id(3)

    @pl.when(kt == 0)
    def _():
        m_scr[...] = jnp.full(m_scr.shape, -jnp.inf, F32)
        l_scr[...] = jnp.zeros(l_scr.shape, F32)
        acc_scr[...] = jnp.zeros(acc_scr.shape, F32)

    @pl.when((qt != n_kt - 1) | (kt == n_kt - 1))
    def _():
        q = _stack_heads(q_ref)
        s = lax.dot_general(q, k_ref[...], (((1,), (1,)), ((), ())), preferred_element_type=F32)
        m_prev = m_scr[...]
        m_new = jnp.maximum(m_prev, jnp.max(s, axis=-1, keepdims=True))
        alpha = jnp.exp(m_prev - m_new)
        p = jnp.exp(s - m_new)
        l_scr[...] = alpha * l_scr[...] + jnp.sum(p, axis=-1, keepdims=True)
        acc_scr[...] = alpha * acc_scr[...] + jnp.dot(p.astype(BF16), v_ref[...], preferred_element_type=F32)
        m_scr[...] = m_new

    @pl.when(kt == n_kt - 1)
    def _():
        o = acc_scr[...] / l_scr[...]
        for r in range(REP):
            o_ref[:, r * HEAD_DIM:(r + 1) * HEAD_DIM] = o[r * TM:(r + 1) * TM, :].astype(BF16)


def _ga_call(q, k, v, nb, tpb):
    n = q.shape[0]

    def kv_map(b, g, qt, kt):
        return (b * tpb + jnp.where(qt == tpb - 1, tpb - 1, kt), g)

    return pl.pallas_call(
        functools.partial(_ga_kernel, n_kt=tpb),
        grid=(nb, N_KV_HEADS, tpb, tpb),
        in_specs=[
            pl.BlockSpec((TM, REP * HEAD_DIM), lambda b, g, qt, kt: (b * tpb + qt, g)),
            pl.BlockSpec((TM, HEAD_DIM), kv_map),
            pl.BlockSpec((TM, HEAD_DIM), kv_map),
        ],
        out_specs=pl.BlockSpec((TM, REP * HEAD_DIM), lambda b, g, qt, kt: (b * tpb + qt, g)),
        out_shape=jax.ShapeDtypeStruct((n, N_HEADS * HEAD_DIM), BF16),
        scratch_shapes=[pltpu.VMEM((REP * TM, 1), F32), pltpu.VMEM((REP * TM, 1), F32),
                        pltpu.VMEM((REP * TM, HEAD_DIM), F32)],
        compiler_params=_cparams(("parallel", "parallel", "parallel", "arbitrary")),
        name="ga_attn",
    )(q, k, v)


SWA_BLOCK = 128


def _swa_kernel(sink_ref, q_ref, kp_ref, kc_ref, kn_ref, kx_ref, vp_ref, vc_ref, vn_ref, vx_ref, o_ref,
                *, n_lat_blocks, seq):
    g = pl.program_id(1)
    i = pl.program_id(2)
    k_all = jnp.concatenate([kp_ref[...], kc_ref[...], kn_ref[...], kx_ref[...]], axis=0)
    v_all = jnp.concatenate([vp_ref[...], vc_ref[...], vn_ref[...], vx_ref[...]], axis=0)
    nk = k_all.shape[0]
    nwin = 3 * SWA_BLOCK
    col = lax.broadcasted_iota(jnp.int32, (SWA_BLOCK, nk), 1)
    row = lax.broadcasted_iota(jnp.int32, (SWA_BLOCK, nk), 0)
    qbase = jnp.where(i < n_lat_blocks, i * SWA_BLOCK, -(1 << 20))
    qpos = qbase + row
    kpos = (i - 1) * SWA_BLOCK + col
    in_win = (kpos >= 0) & (kpos < seq) & (jnp.abs(qpos - kpos) <= WINDOW)
    mask = in_win | (col >= nwin)
    for r in range(REP):
        sl = slice(r * HEAD_DIM, (r + 1) * HEAD_DIM)
        s = lax.dot_general(q_ref[:, sl], k_all, (((1,), (1,)), ((), ())), preferred_element_type=F32)
        s = jnp.where(mask, s, -jnp.inf)
        sk = sink_ref[g * REP + r]
        m = jnp.maximum(jnp.max(s, axis=-1, keepdims=True), sk)
        p = jnp.exp(s - m)
        den = jnp.sum(p, axis=-1, keepdims=True) + jnp.exp(sk - m)
        o = jnp.dot(p.astype(BF16), v_all, preferred_element_type=F32) / den
        o_ref[:, sl] = o.astype(BF16)


def _swa_call(q, k, v, sink, nb, seq, ctx_len):
    n = q.shape[0]
    nt = seq + ctx_len
    bpb = nt // SWA_BLOCK
    nlb = seq // SWA_BLOCK
    cpb = nt // ctx_len

    def win_map(off):
        def f(b, g, i):
            return (b * bpb + jnp.clip(i + off, 0, nlb - 1), g)
        return f

    def ctx_map(b, g, i):
        return (b * cpb + cpb - 1, g)

    kv_specs = [pl.BlockSpec((SWA_BLOCK, HEAD_DIM), win_map(-1)),
                pl.BlockSpec((SWA_BLOCK, HEAD_DIM), win_map(0)),
                pl.BlockSpec((SWA_BLOCK, HEAD_DIM), win_map(1)),
                pl.BlockSpec((ctx_len, HEAD_DIM), ctx_map)]
    return pl.pallas_call(
        functools.partial(_swa_kernel, n_lat_blocks=nlb, seq=seq),
        grid=(nb, N_KV_HEADS, bpb),
        in_specs=[pl.BlockSpec(memory_space=pltpu.SMEM),
                  pl.BlockSpec((SWA_BLOCK, REP * HEAD_DIM), lambda b, g, i: (b * bpb + i, g))]
                 + kv_specs + kv_specs,
        out_specs=pl.BlockSpec((SWA_BLOCK, REP * HEAD_DIM), lambda b, g, i: (b * bpb + i, g)),
        out_shape=jax.ShapeDtypeStruct((n, N_HEADS * HEAD_DIM), BF16),
        compiler_params=_cparams(("parallel", "parallel", "parallel")),
        name="swa_attn",
    )(sink, q, k, k, k, k, v, v, v, v)


def _s5_kernel(x_ref, mt_ref, p_ref, q_ref, a_ref, y_ref, s_scr, h_scr, *, n_chunks, n_lat_chunks, nb):
    half = S5_STATE
    x = x_ref[0]
    s_scr[...] = jnp.dot(x, p_ref[0], preferred_element_type=F32)
    ar = jnp.broadcast_to(a_ref[0, 0:1, :], (nb, 2 * half))
    ai = jnp.broadcast_to(a_ref[0, 1:2, :], (nb, 2 * half))
    is_fwd = lax.broadcasted_iota(jnp.int32, (nb, 2 * half), 1) < half
    hr = jnp.zeros((nb, 2 * half), F32)
    hi = jnp.zeros((nb, 2 * half), F32)
    for step in range(n_chunks):
        rf = nb * ((step + n_lat_chunks) % n_chunks)
        rb = nb * (n_chunks - 1 - step)
        h_scr[rf:rf + nb, 0:half] = hr[:, 0:half]
        h_scr[rf:rf + nb, 2 * half:3 * half] = hi[:, 0:half]
        h_scr[rb:rb + nb, half:2 * half] = hr[:, half:]
        h_scr[rb:rb + nb, 3 * half:] = hi[:, half:]
        sr = jnp.where(is_fwd, s_scr[rf:rf + nb, 0:2 * half], s_scr[rb:rb + nb, 0:2 * half])
        si = jnp.where(is_fwd, s_scr[rf:rf + nb, 2 * half:], s_scr[rb:rb + nb, 2 * half:])
        hr, hi = ar * hr - ai * hi + sr, ar * hi + ai * hr + si
    y = jnp.dot(x, mt_ref[0], preferred_element_type=F32)
    y_ref[0] = y + jnp.dot(h_scr[...].astype(BF16), q_ref[0], preferred_element_type=F32)


def _s5_core_call(xg, mt, pm, qm, a, n_chunks, n_lat_chunks, nb):
    g, rows, w = xg.shape
    return pl.pallas_call(
        functools.partial(_s5_kernel, n_chunks=n_chunks, n_lat_chunks=n_lat_chunks, nb=nb),
        grid=(g,),
        in_specs=[
            pl.BlockSpec((1, rows, w), lambda i: (i, 0, 0)),
            pl.BlockSpec((1, w, w), lambda i: (i, 0, 0)),
            pl.BlockSpec((1, w, 4 * S5_STATE), lambda i: (i, 0, 0)),
            pl.BlockSpec((1, 4 * S5_STATE, w), lambda i: (i, 0, 0)),
            pl.BlockSpec((1, 2, 2 * S5_STATE), lambda i: (i, 0, 0)),
        ],
        out_specs=pl.BlockSpec((1, rows, w), lambda i: (i, 0, 0)),
        out_shape=jax.ShapeDtypeStruct((g, rows, w), F32),
        scratch_shapes=[pltpu.VMEM((rows, 4 * S5_STATE), F32), pltpu.VMEM((rows, 4 * S5_STATE), F32)],
        compiler_params=_cparams(("parallel",)),
        name="s5_core",
    )(xg, mt, pm, qm, a)


def _s5_weights(lam_re, lam_im, log_dt, b_re, b_im, c_re, c_im):
    t = S5_CHUNK
    lam = lax.complex(lam_re.astype(F32), lam_im.astype(F32))
    dt = jnp.exp(log_dt.astype(F32))[..., None]
    ldt = lam * dt
    lam_bar = jnp.exp(ldt)
    b_bar = ((lam_bar - 1) / lam)[..., None] * lax.complex(b_re.astype(F32), b_im.astype(F32))
    cm = lax.complex(c_re.astype(F32), c_im.astype(F32))
    tau = jnp.arange(t + 1, dtype=F32)
    pw = jnp.exp(ldt[:, None] * tau[None, :, None, None])
    hp = lax.Precision.HIGHEST
    taps = jnp.einsum('dgop,dtgp,dgpi->dtgoi', cm, pw[:, :t], b_bar, precision=hp).real
    tt = jnp.arange(t)
    lag = tt[:, None] - tt[None, :]
    kf = jnp.where((lag >= 0)[:, :, None, None, None], taps[0][jnp.clip(lag, 0, t - 1)], 0.0)
    kb = jnp.where((lag <= 0)[:, :, None, None, None], taps[1][jnp.clip(-lag, 0, t - 1)], 0.0)
    g = lam_re.shape[1]
    mt = (kf + kb).transpose(2, 1, 4, 0, 3).reshape(g, t * S5_GROUP, t * S5_GROUP)
    pf = pw[0, :t][::-1][:, :, :, None] * b_bar[0][None]
    pb = pw[1, :t][:, :, :, None] * b_bar[1][None]
    pm = jnp.concatenate([pf.real, pb.real, pf.imag, pb.imag], axis=2)
    pm = pm.transpose(1, 0, 3, 2).reshape(g, t * S5_GROUP, 4 * S5_STATE)
    qf = cm[0][None] * pw[0, 1:t + 1][:, :, None, :]
    qb = cm[1][None] * pw[1, 1:t + 1][::-1][:, :, None, :]
    qm = jnp.concatenate([qf.real, qb.real, -qf.imag, -qb.imag], axis=3)
    qm = qm.transpose(1, 3, 0, 2).reshape(g, 4 * S5_STATE, t * S5_GROUP)
    at = pw[:, t]
    a = jnp.stack([jnp.concatenate([at[0].real, at[1].real], axis=-1),
                   jnp.concatenate([at[0].imag, at[1].imag], axis=-1)], axis=1)
    return mt.astype(BF16), pm.astype(BF16), qm.astype(BF16), a.astype(F32)


def _gelu_tanh(x):
    return 0.5 * x * (1.0 + jnp.tanh(math.sqrt(2.0 / math.pi) * (x + 0.044715 * (x * x * x))))


def _glu_kernel(h_ref, y_ref, g_ref, mod_ref, d_ref, w_ref, b_ref, o_ref):
    h = h_ref[...]
    u = _norm_mod(h, g_ref[...], _mod_row(mod_ref, 0), _mod_row(mod_ref, 1))
    z = _gelu_tanh(y_ref[...] + d_ref[...] * u)
    acc = jnp.dot(z.astype(BF16), w_ref[...], preferred_element_type=F32) + b_ref[...]
    o_ref[...] = h + _mod_row(mod_ref, 2) * (z * _sigmoid(acc))


def _glu_call(h, y, g, mod, dvec, w_glu_b, b_glu, tpb, nb):
    n, d = h.shape
    return pl.pallas_call(
        _glu_kernel,
        grid=(n // TM,),
        in_specs=[_row_spec(d, tpb), _row_spec(d, tpb), _const_spec((1, d)), _mod_spec(d, tpb, nb),
                  _const_spec((1, d)), _const_spec((d, d)), _const_spec((1, d))],
        out_specs=_row_spec(d, tpb),
        out_shape=jax.ShapeDtypeStruct((n, d), F32),
        compiler_params=_cparams(("parallel",)),
        name="s5_glu",
    )(h, y, g.reshape(1, d), mod, dvec.reshape(1, d), w_glu_b, b_glu.reshape(1, d))


def _router_kernel(h_ref, g_ref, mod_ref, wr_ref, br_ref, v_ref, ids_ref, gate_ref):
    v = _norm_mod(h_ref[...], g_ref[...], _mod_row(mod_ref, 3), _mod_row(mod_ref, 4))
    v_ref[...] = v
    lg = jnp.dot(v, wr_ref[...], preferred_element_type=F32, precision=lax.Precision.HIGHEST) + br_ref[...]
    lane = lax.broadcasted_iota(jnp.int32, lg.shape, 1)
    neg = -jnp.inf
    big = jnp.int32(LANES)
    gl = jnp.where(lane < N_GROUPS, lg, neg)
    gmax = jnp.max(gl, axis=-1, keepdims=True)
    gidx = jnp.min(jnp.where(gl == gmax, lane, big), axis=-1, keepdims=True)
    g_w = 1.0 / jnp.sum(jnp.exp(gl - gmax), axis=-1, keepdims=True)
    lo = N_GROUPS + EXPERTS_PER_GROUP * gidx
    el = jnp.where((lane >= lo) & (lane < lo + EXPERTS_PER_GROUP), lg, neg)
    m1 = jnp.max(el, axis=-1, keepdims=True)
    i1 = jnp.min(jnp.where(el == m1, lane, big), axis=-1, keepdims=True)
    el2 = jnp.where(lane == i1, neg, el)
    m2 = jnp.max(el2, axis=-1, keepdims=True)
    i2 = jnp.min(jnp.where(el2 == m2, lane, big), axis=-1, keepdims=True)
    dd = jnp.exp(m2 - m1)
    w1 = g_w / (1.0 + dd)
    w2 = g_w * dd / (1.0 + dd)
    ids_ref[...] = jnp.where(lane == 0, i1 - N_GROUPS, jnp.where(lane == 1, i2 - N_GROUPS, 0))
    gate_ref[...] = jnp.where(lane == 0, w1, jnp.where(lane == 1, w2, 0.0))


def _router_call(h, g, mod, wr, br, tpb, nb):
    n, d = h.shape
    return pl.pallas_call(
        _router_kernel,
        grid=(n // TM,),
        in_specs=[_row_spec(d, tpb), _const_spec((1, d)), _mod_spec(d, tpb, nb),
                  _const_spec((d, LANES)), _const_spec((1, LANES))],
        out_specs=[_row_spec(d, tpb), _row_spec(LANES, tpb), _row_spec(LANES, tpb)],
        out_shape=[jax.ShapeDtypeStruct((n, d), F32), jax.ShapeDtypeStruct((n, LANES), jnp.int32),
                   jax.ShapeDtypeStruct((n, LANES), F32)],
        compiler_params=_cparams(("parallel",)),
        name="moe_router",
    )(h, g.reshape(1, d), mod, wr, br)


def _gather_rows(idx_ref, n_rows, src_hbm, dst_ref, sem):
    def body(r, carry):
        t = idx_ref[0, 0, r]
        pltpu.make_async_copy(src_hbm.at[pl.ds(t, 1), :], dst_ref.at[pl.ds(r, 1), :], sem).start()
        return carry
    lax.fori_loop(0, n_rows, body, 0, unroll=8)


def _wait_rows(src_hbm, dst_ref, sem):
    n_rows = dst_ref.shape[0]
    pltpu.make_async_copy(src_hbm.at[pl.ds(0, n_rows), :], dst_ref, sem).wait()


def _ffn_kernel(be_ref, nu_ref, tok0_ref, tokn_ref, v_hbm, w1_ref, w3_ref, w2_ref, y_ref,
                xbuf, sem, w1b, w3b, w2b):
    b = pl.program_id(0)
    n_used = nu_ref[0]
    slot = b % 2

    @pl.when(b == 0)
    def _():
        _gather_rows(tok0_ref, MOE_TILE, v_hbm, xbuf.at[0], sem.at[0])

    @pl.when(b + 1 < n_used)
    def _():
        _gather_rows(tokn_ref, MOE_TILE, v_hbm, xbuf.at[1 - slot], sem.at[1 - slot])

    new_expert = (b == 0) | (be_ref[b] != be_ref[jnp.maximum(b - 1, 0)])

    @pl.when(new_expert & (b < n_used))
    def _():
        w1b[...] = w1_ref[0].astype(BF16)
        w3b[...] = w3_ref[0].astype(BF16)
        w2b[...] = w2_ref[0].astype(BF16)

    @pl.when(b < n_used)
    def _():
        _wait_rows(v_hbm, xbuf.at[slot], sem.at[slot])
        x = xbuf[slot].astype(BF16)
        h1 = jnp.dot(x, w1b[...], preferred_element_type=F32)
        h3 = jnp.dot(x, w3b[...], preferred_element_type=F32)
        act = (h1 * _sigmoid(h1)) * h3
        y_ref[...] = jnp.dot(act.astype(BF16), w2b[...], preferred_element_type=F32)

    @pl.when(b >= n_used)
    def _():
        y_ref[...] = jnp.zeros(y_ref.shape, F32)


def _ffn_call(v, slot_tok, block_e, n_used, w1, w3, w2):
    n, d = v.shape
    ne, _, de = w1.shape
    n_blocks = slot_tok.shape[0] // MOE_TILE
    tok3 = slot_tok.reshape(n_blocks, 1, MOE_TILE)
    grid_spec = pltpu.PrefetchScalarGridSpec(
        num_scalar_prefetch=2,
        grid=(n_blocks,),
        in_specs=[
            pl.BlockSpec((1, 1, MOE_TILE), lambda b, be, nu: (0, 0, 0), memory_space=pltpu.SMEM),
            pl.BlockSpec((1, 1, MOE_TILE), lambda b, be, nu: (jnp.minimum(b + 1, n_blocks - 1), 0, 0),
                         memory_space=pltpu.SMEM),
            pl.BlockSpec(memory_space=pl.ANY),
            pl.BlockSpec((1, d, de), lambda b, be, nu: (be[b], 0, 0)),
            pl.BlockSpec((1, d, de), lambda b, be, nu: (be[b], 0, 0)),
            pl.BlockSpec((1, de, d), lambda b, be, nu: (be[b], 0, 0)),
        ],
        out_specs=pl.BlockSpec((MOE_TILE, d), lambda b, be, nu: (b, 0)),
        scratch_shapes=[pltpu.VMEM((2, MOE_TILE, d), F32), pltpu.SemaphoreType.DMA((2,)),
                        pltpu.VMEM((d, de), BF16), pltpu.VMEM((d, de), BF16), pltpu.VMEM((de, d), BF16)],
    )
    return pl.pallas_call(
        _ffn_kernel,
        grid_spec=grid_spec,
        out_shape=jax.ShapeDtypeStruct((n_blocks * MOE_TILE, d), F32),
        compiler_params=_cparams(("arbitrary",)),
        name="moe_ffn",
    )(block_e, n_used, tok3, tok3, v, w1, w3, w2)


def _combine_kernel(s0_ref, sn_ref, yb_hbm, h_ref, gate_ref, mod_ref, o_ref, ybuf, sem):
    i = pl.program_id(0)
    n_steps = pl.num_programs(0)
    slot = i % 2

    @pl.when(i == 0)
    def _():
        _gather_rows(s0_ref, TOP_K * TM, yb_hbm, ybuf.at[0], sem.at[0])

    @pl.when(i + 1 < n_steps)
    def _():
        _gather_rows(sn_ref, TOP_K * TM, yb_hbm, ybuf.at[1 - slot], sem.at[1 - slot])

    _wait_rows(yb_hbm, ybuf.at[slot], sem.at[slot])
    gate = gate_ref[...]
    y = gate[:, 0:1] * ybuf[slot, 0:TM, :] + gate[:, 1:2] * ybuf[slot, TM:2 * TM, :]
    o_ref[...] = h_ref[...] + _mod_row(mod_ref, 5) * y


def _combine_call(h, yb, slots_kt, gates, mod, tpb, nb):
    n, d = h.shape
    n_tiles = n // TM
    return pl.pallas_call(
        _combine_kernel,
        grid=(n_tiles,),
        in_specs=[
            pl.BlockSpec((1, 1, TOP_K * TM), lambda i: (0, 0, 0), memory_space=pltpu.SMEM),
            pl.BlockSpec((1, 1, TOP_K * TM), lambda i: (jnp.minimum(i + 1, n_tiles - 1), 0, 0),
                         memory_space=pltpu.SMEM),
            pl.BlockSpec(memory_space=pl.ANY),
            _row_spec(d, tpb), _row_spec(LANES, tpb), _mod_spec(d, tpb, nb),
        ],
        out_specs=_row_spec(d, tpb),
        out_shape=jax.ShapeDtypeStruct((n, d), F32),
        scratch_shapes=[pltpu.VMEM((2, TOP_K * TM, d), F32), pltpu.SemaphoreType.DMA((2,))],
        compiler_params=_cparams(("arbitrary",)),
        name="moe_combine",
    )(slots_kt, slots_kt, yb, h, gates, mod)


def _routing_tables(ids, n_tok):
    n = n_tok * TOP_K
    expert = ids[:, :TOP_K].reshape(n)
    order = jnp.argsort(expert).astype(jnp.int32)
    counts = jnp.zeros((N_EXPERTS,), jnp.int32).at[expert].add(1)
    padded = ((counts + MOE_TILE - 1) // MOE_TILE) * MOE_TILE
    pend = jnp.cumsum(padded)
    pstart = pend - padded
    sstart = jnp.cumsum(counts) - counts
    n_blocks = n // MOE_TILE + N_EXPERTS
    cap = n_blocks * MOE_TILE
    block_e = jnp.minimum(jnp.searchsorted(pend, jnp.arange(n_blocks, dtype=jnp.int32) * MOE_TILE, side='right'),
                          N_EXPERTS - 1).astype(jnp.int32)
    pos = jnp.arange(cap, dtype=jnp.int32)
    e_of = block_e[pos // MOE_TILE]
    r = pos - pstart[e_of]
    valid = (r < counts[e_of]) & (pos < pend[-1])
    sorted_idx = jnp.clip(sstart[e_of] + r, 0, n - 1)
    slot_tok = jnp.where(valid, order[sorted_idx] // TOP_K, 0).astype(jnp.int32)
    inv = jnp.zeros((n,), jnp.int32).at[order].set(jnp.arange(n, dtype=jnp.int32))
    slot_of = (pstart[expert] + inv - sstart[expert]).astype(jnp.int32).reshape(n_tok, TOP_K)
    n_tiles = n_tok // TM
    slots_kt = slot_of.reshape(n_tiles, TM, TOP_K).transpose(0, 2, 1).reshape(n_tiles, 1, TOP_K * TM)
    n_used = (pend[-1] // MOE_TILE).astype(jnp.int32).reshape(1)
    return slot_tok, block_e, n_used, slots_kt


def _moe_layer(h, g2n, mod, w_grp, b_grp, w_rt, b_rt, w1, w3, w2, tpb, nb):
    n, d = h.shape
    pad = LANES - N_GROUPS - N_EXPERTS
    wr = jnp.concatenate([w_grp, w_rt, jnp.zeros((d, pad), F32)], axis=1)
    br = jnp.concatenate([b_grp, b_rt, jnp.zeros((pad,), F32)]).reshape(1, LANES)
    v, ids, gates = _router_call(h, g2n, mod, wr, br, tpb, nb)
    slot_tok, block_e, n_used, slots_kt = _routing_tables(ids, n)
    yb = _ffn_call(v, slot_tok, block_e, n_used, w1, w3, w2)
    return _combine_call(h, yb, slots_kt, gates, mod, tpb, nb)


def _s5_layer(h, g1n, mod, prm, tpb, nb):
    n, d = h.shape
    nt = n // nb
    n_chunks = nt // S5_CHUNK
    groups = d // S5_GROUP
    u = _norm_mod_call(h, g1n, mod, tpb, nb)
    xg = u.reshape(nb, n_chunks, S5_CHUNK, groups, S5_GROUP).transpose(3, 1, 0, 2, 4)
    xg = xg.reshape(groups, n_chunks * nb, S5_CHUNK * S5_GROUP)
    mt, pm, qm, a = _s5_weights(prm['lam_re'], prm['lam_im'], prm['log_dt'], prm['b_re'], prm['b_im'],
                                prm['c_re'], prm['c_im'])
    n_lat_chunks = (nt - prm['ctx_len']) // S5_CHUNK
    yg = _s5_core_call(xg, mt, pm, qm, a, n_chunks, n_lat_chunks, nb)
    y = yg.reshape(groups, n_chunks, nb, S5_CHUNK, S5_GROUP).transpose(2, 1, 3, 0, 4).reshape(n, d)
    return _glu_call(h, y, g1n, mod, prm['d'], prm['w_glu'].astype(BF16), prm['b_glu'], tpb, nb)


def _rope_tables(seq, ctx_len):
    rows = seq // GRID_W
    t_row = jnp.repeat(jnp.arange(rows), GRID_W).astype(F32)
    t_col = jnp.tile(jnp.arange(GRID_W), rows).astype(F32)
    half = HEAD_DIM // 2
    inv = ROPE_BASE ** (-jnp.arange(0, half, 2, dtype=F32) / half)
    ar = t_row[:, None] * inv
    ac = t_col[:, None] * inv
    ang = jnp.concatenate([ar, ar, ac, ac], axis=-1)
    cos = jnp.cos(ang)
    sin = jnp.sin(ang)
    lane = jnp.arange(HEAD_DIM)
    sin = jnp.where((lane & 32) == 0, -sin, sin)
    cos = jnp.concatenate([cos, jnp.ones((ctx_len, HEAD_DIM), F32)], axis=0)
    sin = jnp.concatenate([sin, jnp.zeros((ctx_len, HEAD_DIM), F32)], axis=0)
    return cos, sin


def kernel(x, c, ctx, c_ctx, w_mod, b_mod, norm1_g, norm2_g, s5_lam_re, s5_lam_im, s5_log_dt, s5_b_re, s5_b_im, s5_c_re, s5_c_im, s5_d, s5_w_glu, s5_b_glu, swa_w_qkv, swa_q_g, swa_k_g, swa_sink, swa_w_o, ga_w_qkv, ga_q_g, ga_k_g, ga_w_o, moe_w_grp, moe_b_grp, moe_w_rt, moe_b_rt, moe_w1, moe_w3, moe_w2):
    nb, seq, d = x.shape
    ctx_len = ctx.shape[1]
    depth = w_mod.shape[0]
    nt = seq + ctx_len
    n = nb * nt
    tpb = nt // TM
    assert nt % TM == 0 and ctx_len == TM and nb < 8 and seq % SWA_BLOCK == 0 and nt % S5_CHUNK == 0

    c8 = jnp.zeros((8, d), F32).at[:nb].set(c).at[nb].set(c_ctx)
    mods = _ada_mod_all(c8, w_mod, b_mod).reshape(depth, 8, N_MOD, d)
    cos, sin = _rope_tables(seq, ctx_len)

    h = jnp.concatenate([x, ctx], axis=1).reshape(n, d)
    for i in range(depth):
        kind, j = i % 3, i // 3
        mod = mods[i]
        if kind == 0:
            prm = dict(lam_re=s5_lam_re[j], lam_im=s5_lam_im[j], log_dt=s5_log_dt[j], b_re=s5_b_re[j],
                       b_im=s5_b_im[j], c_re=s5_c_re[j], c_im=s5_c_im[j], d=s5_d[j], w_glu=s5_w_glu[j],
                       b_glu=s5_b_glu[j], ctx_len=ctx_len)
            h = _s5_layer(h, norm1_g[i], mod, prm, tpb, nb)
        elif kind == 1:
            q, k, v = _qkv_call(h, norm1_g[i], mod, cos, sin, swa_q_g[j], swa_k_g[j],
                                swa_w_qkv[j].astype(BF16), tpb, nb)
            o = _swa_call(q, k, v, swa_sink[j], nb, seq, ctx_len)
            h = _oproj_call(o, swa_w_o[j].astype(BF16), h, mod, tpb, nb)
        else:
            q, k, v = _qkv_call(h, norm1_g[i], mod, cos, sin, ga_q_g[j], ga_k_g[j],
                                ga_w_qkv[j].astype(BF16), tpb, nb)
            o = _ga_call(q, k, v, nb, tpb)
            h = _oproj_call(o, ga_w_o[j].astype(BF16), h, mod, tpb, nb)
        h = _moe_layer(h, norm2_g[i], mod, moe_w_grp[i], moe_b_grp[i], moe_w_rt[i], moe_b_rt[i],
                       moe_w1[i], moe_w3[i], moe_w2[i], tpb, nb)
    return h.reshape(nb, nt, d)[:, :seq]
```

```python
import functools
import math

import jax
import jax.numpy as jnp
from jax import lax
from jax.experimental import pallas as pl
from jax.experimental.pallas import tpu as pltpu

F32 = jnp.float32
BF16 = jnp.bfloat16

EPS = 1e-6
N_MOD = 6
HEAD_DIM = 128
N_HEADS = 16
N_KV_HEADS = 4
REP = N_HEADS // N_KV_HEADS
WINDOW = 128
GRID_W = 64
ROPE_BASE = 10000.0
S5_GROUP = 16
S5_STATE = 64
N_GROUPS = 4
EXPERTS_PER_GROUP = 8
N_EXPERTS = N_GROUPS * EXPERTS_PER_GROUP
TOP_K = 2
ATTN_SCALE = HEAD_DIM ** -0.5

LANES = 128
TM = 256
S5_CHUNK = 16
MOE_TILE = 256
VMEM_LIMIT = 56 * 1024 * 1024


def _cparams(sem, vmem=VMEM_LIMIT):
    return pltpu.CompilerParams(dimension_semantics=sem, vmem_limit_bytes=vmem)


def _sigmoid(x):
    return 1.0 / (1.0 + jnp.exp(-x))


def _norm_mod(x, g, shift, scale):
    ms = jnp.mean(x * x, axis=-1, keepdims=True)
    y = x * lax.rsqrt(ms + EPS) * g
    return y * (1.0 + scale) + shift


def _mod_row(mod_ref, k):
    return mod_ref[0, k:k + 1, :]


def _mod_kernel(c_ref, w_ref, b_ref, o_ref):
    c = c_ref[...]
    cond = c * _sigmoid(c)
    acc = jnp.dot(cond.astype(BF16), w_ref[...].astype(BF16), preferred_element_type=F32)
    o_ref[...] = acc + b_ref[...]


def _ada_mod_all(c8, w_mod, b_mod):
    depth, d, n6 = w_mod.shape
    tn = 1536
    return pl.pallas_call(
        _mod_kernel,
        grid=(depth, n6 // tn),
        in_specs=[
            pl.BlockSpec((8, d), lambda l, j: (0, 0)),
            pl.BlockSpec((None, d, tn), lambda l, j: (l, 0, j)),
            pl.BlockSpec((None, 1, tn), lambda l, j: (l, 0, j)),
        ],
        out_specs=pl.BlockSpec((None, 8, tn), lambda l, j: (l, 0, j)),
        out_shape=jax.ShapeDtypeStruct((depth, 8, n6), F32),
        compiler_params=_cparams(("parallel", "parallel")),
        name="ada_mod",
    )(c8, w_mod, b_mod.reshape(depth, 1, n6))


def _row_spec(width, tpb):
    del tpb
    return pl.BlockSpec((TM, width), lambda i: (i, 0))


def _mod_spec(d, tpb, nb):
    return pl.BlockSpec((1, N_MOD, d), lambda i: (jnp.where(i % tpb == tpb - 1, nb, i // tpb), 0, 0))


def _const_spec(shape):
    nd = len(shape)
    return pl.BlockSpec(shape, lambda i: (0,) * nd)


def _u_kernel(h_ref, g_ref, mod_ref, o_ref):
    u = _norm_mod(h_ref[...], g_ref[...], _mod_row(mod_ref, 0), _mod_row(mod_ref, 1))
    o_ref[...] = u.astype(o_ref.dtype)


def _norm_mod_call(h, g, mod, tpb, nb):
    n, d = h.shape
    return pl.pallas_call(
        _u_kernel,
        grid=(n // TM,),
        in_specs=[_row_spec(d, tpb), _const_spec((1, d)), _mod_spec(d, tpb, nb)],
        out_specs=_row_spec(d, tpb),
        out_shape=jax.ShapeDtypeStruct((n, d), BF16),
        compiler_params=_cparams(("parallel",)),
        name="norm_mod",
    )(h, g.reshape(1, d), mod)


def _qkv_kernel(h_ref, g_ref, mod_ref, cos_ref, sin_ref, qg_ref, kg_ref, w_ref, q_ref, k_ref, v_ref):
    u = _norm_mod(h_ref[...], g_ref[...], _mod_row(mod_ref, 0), _mod_row(mod_ref, 1))
    p = jnp.dot(u.astype(BF16), w_ref[...], preferred_element_type=F32)
    cos = cos_ref[...]
    sin = sin_ref[...]
    lane = lax.broadcasted_iota(jnp.int32, (TM, HEAD_DIM), 1)
    first = (lane & 32) == 0

    def head(xh, gain, scale):
        ms = jnp.mean(xh * xh, axis=-1, keepdims=True)
        xn = xh * lax.rsqrt(ms + EPS) * gain
        rot = jnp.where(first, pltpu.roll(xn, HEAD_DIM - 32, 1), pltpu.roll(xn, 32, 1))
        return (xn * cos + rot * sin) * scale

    qg = qg_ref[...]
    kg = kg_ref[...]
    nq = N_HEADS * HEAD_DIM
    nk = N_KV_HEADS * HEAD_DIM
    for hh in range(N_HEADS):
        sl = slice(hh * HEAD_DIM, (hh + 1) * HEAD_DIM)
        q_ref[:, sl] = head(p[:, sl], qg, ATTN_SCALE).astype(BF16)
    for hh in range(N_KV_HEADS):
        sl = slice(hh * HEAD_DIM, (hh + 1) * HEAD_DIM)
        k_ref[:, sl] = head(p[:, nq + hh * HEAD_DIM:nq + (hh + 1) * HEAD_DIM], kg, 1.0).astype(BF16)
    v_ref[...] = p[:, nq + nk:].astype(BF16)


def _qkv_call(h, g, mod, cos, sin, q_g, k_g, w_qkv_b, tpb, nb):
    n, d = h.shape
    nq = N_HEADS * HEAD_DIM
    nk = N_KV_HEADS * HEAD_DIM
    return pl.pallas_call(
        _qkv_kernel,
        grid=(n // TM,),
        in_specs=[
            _row_spec(d, tpb), _const_spec((1, d)), _mod_spec(d, tpb, nb),
            pl.BlockSpec((TM, HEAD_DIM), lambda i: (i % tpb, 0)),
            pl.BlockSpec((TM, HEAD_DIM), lambda i: (i % tpb, 0)),
            _const_spec((1, HEAD_DIM)), _const_spec((1, HEAD_DIM)),
            _const_spec((d, nq + 2 * nk)),
        ],
        out_specs=[_row_spec(nq, tpb), _row_spec(nk, tpb), _row_spec(nk, tpb)],
        out_shape=[jax.ShapeDtypeStruct((n, nq), BF16), jax.ShapeDtypeStruct((n, nk), BF16),
                   jax.ShapeDtypeStruct((n, nk), BF16)],
        compiler_params=_cparams(("parallel",)),
        name="qkv_proj",
    )(h, g.reshape(1, d), mod, cos, sin, q_g.reshape(1, HEAD_DIM), k_g.reshape(1, HEAD_DIM), w_qkv_b)


def _oproj_kernel(a_ref, w_ref, h_ref, mod_ref, o_ref):
    acc = jnp.dot(a_ref[...], w_ref[...], preferred_element_type=F32)
    o_ref[...] = h_ref[...] + _mod_row(mod_ref, 2) * acc


def _oproj_call(a, w_o_b, h, mod, tpb, nb):
    n, d = h.shape
    return pl.pallas_call(
        _oproj_kernel,
        grid=(n // TM,),
        in_specs=[_row_spec(a.shape[1], tpb), _const_spec(w_o_b.shape), _row_spec(d, tpb), _mod_spec(d, tpb, nb)],
        out_specs=_row_spec(d, tpb),
        out_shape=jax.ShapeDtypeStruct((n, d), F32),
        compiler_params=_cparams(("parallel",)),
        name="o_proj",
    )(a, w_o_b, h, mod)


GA_TQ = 256
GA_TK = 512
SWA_TQ = 256


def _stack_heads(q_ref):
    return jnp.concatenate([q_ref[:, r * HEAD_DIM:(r + 1) * HEAD_DIM] for r in range(REP)], axis=0)


def _unstack_heads(o, o_ref):
    tq = o_ref.shape[0]
    for r in range(REP):
        o_ref[:, r * HEAD_DIM:(r + 1) * HEAD_DIM] = o[r * tq:(r + 1) * tq, :].astype(o_ref.dtype)


def _with_ones(v):
    return jnp.concatenate([v, jnp.ones_like(v)], axis=1)


def _flash_step(q, k, v, m, acc):
    s = lax.dot_general(q, k, (((1,), (1,)), ((), ())), preferred_element_type=F32)
    m_new = jnp.maximum(m, jnp.max(s, axis=-1, keepdims=True))
    alpha = jnp.exp(m - m_new)
    p = jnp.exp(s - m_new[:, :1]).astype(BF16)
    pv = jnp.dot(p, _with_ones(v), preferred_element_type=F32)
    return m_new, jnp.concatenate([alpha, alpha], axis=1) * acc + pv


def _ga_kernel(q_ref, k_ref, v_ref, o_ref, *, n_lat_tiles, seq):
    qt = pl.program_id(2)
    q = _stack_heads(q_ref)
    rows = q.shape[0]
    m0 = jnp.full((rows, HEAD_DIM), -jnp.inf, F32)
    a0 = jnp.zeros((rows, 2 * HEAD_DIM), F32)

    def finish(acc):
        _unstack_heads(acc[:, :HEAD_DIM] / acc[:, HEAD_DIM:], o_ref)

    @pl.when(qt < n_lat_tiles)
    def _():
        m, acc = m0, a0
        for c in range(seq // GA_TK):
            m, acc = _flash_step(q, k_ref[c * GA_TK:(c + 1) * GA_TK, :], v_ref[c * GA_TK:(c + 1) * GA_TK, :], m, acc)
        m, acc = _flash_step(q, k_ref[seq:, :], v_ref[seq:, :], m, acc)
        finish(acc)

    @pl.when(qt >= n_lat_tiles)
    def _():
        finish(_flash_step(q, k_ref[seq:, :], v_ref[seq:, :], m0, a0)[1])


def _attn_specs(tq, nt):
    n_qt = nt // tq
    q_spec = pl.BlockSpec((tq, REP * HEAD_DIM), lambda b, g, i: (b * n_qt + i, g))
    kv_spec = pl.BlockSpec((nt, HEAD_DIM), lambda b, g, i: (b, g))
    return n_qt, q_spec, kv_spec


def _ga_call(q, k, v, nb, seq, ctx_len):
    n = q.shape[0]
    nt = seq + ctx_len
    assert seq % GA_TK == 0 and seq % GA_TQ == 0 and ctx_len % GA_TQ == 0
    n_qt, q_spec, kv_spec = _attn_specs(GA_TQ, nt)
    return pl.pallas_call(
        functools.partial(_ga_kernel, n_lat_tiles=seq // GA_TQ, seq=seq),
        grid=(nb, N_KV_HEADS, n_qt),
        in_specs=[q_spec, kv_spec, kv_spec],
        out_specs=q_spec,
        out_shape=jax.ShapeDtypeStruct((n, N_HEADS * HEAD_DIM), BF16),
        compiler_params=_cparams(("parallel", "parallel", "parallel")),
        name="ga_attn",
    )(q, k, v)


def _swa_kernel(sink_ref, q_ref, k_ref, v_ref, o_ref, *, n_lat_tiles, seq):
    g = pl.program_id(1)
    i = pl.program_id(2)
    tq = q_ref.shape[0]
    nwin = tq + 2 * WINDOW
    q = _stack_heads(q_ref)
    rows = q.shape[0]
    start = pl.multiple_of(jnp.clip(i * tq - WINDOW, 0, seq - nwin), WINDOW)
    k_all = jnp.concatenate([k_ref[pl.ds(start, nwin), :], k_ref[seq:, :]], axis=0)
    v_all = jnp.concatenate([v_ref[pl.ds(start, nwin), :], v_ref[seq:, :]], axis=0)
    nk = k_all.shape[0]
    col = lax.broadcasted_iota(jnp.int32, (rows, nk), 1)
    row = lax.broadcasted_iota(jnp.int32, (rows, nk), 0)
    qbase = jnp.where(i < n_lat_tiles, i * tq, -(1 << 20))
    qpos = qbase + (row & (tq - 1))
    mask = (jnp.abs(qpos - (start + col)) <= WINDOW) | (col >= nwin)
    s = lax.dot_general(q, k_all, (((1,), (1,)), ((), ())), preferred_element_type=F32)
    s = jnp.where(mask, s, -jnp.inf)
    head = lax.broadcasted_iota(jnp.int32, (rows, HEAD_DIM), 0) // tq
    sk = jnp.zeros((rows, HEAD_DIM), F32)
    for r in range(REP):
        sk = jnp.where(head == r, sink_ref[g * REP + r], sk)
    m = jnp.maximum(jnp.max(s, axis=-1, keepdims=True), sk)
    p = jnp.exp(s - m[:, :1]).astype(BF16)
    acc = jnp.dot(p, _with_ones(v_all), preferred_element_type=F32)
    _unstack_heads(acc[:, :HEAD_DIM] / (acc[:, HEAD_DIM:] + jnp.exp(sk - m)), o_ref)


def _swa_call(q, k, v, sink, nb, seq, ctx_len):
    n = q.shape[0]
    nt = seq + ctx_len
    assert seq % SWA_TQ == 0 and ctx_len % SWA_TQ == 0 and SWA_TQ % WINDOW == 0 and SWA_TQ & (SWA_TQ - 1) == 0
    n_qt, q_spec, kv_spec = _attn_specs(SWA_TQ, nt)
    return pl.pallas_call(
        functools.partial(_swa_kernel, n_lat_tiles=seq // SWA_TQ, seq=seq),
        grid=(nb, N_KV_HEADS, n_qt),
        in_specs=[pl.BlockSpec(memory_space=pltpu.SMEM), q_spec, kv_spec, kv_spec],
        out_specs=q_spec,
        out_shape=jax.ShapeDtypeStruct((n, N_HEADS * HEAD_DIM), BF16),
        compiler_params=_cparams(("parallel", "parallel", "parallel")),
        name="swa_attn",
    )(sink, q, k, v)


def _s5_kernel(x_ref, mt_ref, p_ref, q_ref, a_ref, y_ref, s_scr, h_scr, *, n_chunks, n_lat_chunks, nb):
    half = S5_STATE
    x = x_ref[0]
    s_scr[...] = jnp.dot(x, p_ref[0], preferred_element_type=F32)
    ar = jnp.broadcast_to(a_ref[0, 0:1, :], (nb, 2 * half))
    ai = jnp.broadcast_to(a_ref[0, 1:2, :], (nb, 2 * half))
    is_fwd = lax.broadcasted_iota(jnp.int32, (nb, 2 * half), 1) < half
    hr = jnp.zeros((nb, 2 * half), F32)
    hi = jnp.zeros((nb, 2 * half), F32)
    for step in range(n_chunks):
        rf = nb * ((step + n_lat_chunks) % n_chunks)
        rb = nb * (n_chunks - 1 - step)
        h_scr[rf:rf + nb, 0:half] = hr[:, 0:half]
        h_scr[rf:rf + nb, 2 * half:3 * half] = hi[:, 0:half]
        h_scr[rb:rb + nb, half:2 * half] = hr[:, half:]
        h_scr[rb:rb + nb, 3 * half:] = hi[:, half:]
        sr = jnp.where(is_fwd, s_scr[rf:rf + nb, 0:2 * half], s_scr[rb:rb + nb, 0:2 * half])
        si = jnp.where(is_fwd, s_scr[rf:rf + nb, 2 * half:], s_scr[rb:rb + nb, 2 * half:])
        hr, hi = ar * hr - ai * hi + sr, ar * hi + ai * hr + si
    y = jnp.dot(x, mt_ref[0], preferred_element_type=F32)
    y_ref[0] = y + jnp.dot(h_scr[...].astype(BF16), q_ref[0], preferred_element_type=F32)


def _s5_core_call(xg, mt, pm, qm, a, n_chunks, n_lat_chunks, nb):
    g, rows, w = xg.shape
    return pl.pallas_call(
        functools.partial(_s5_kernel, n_chunks=n_chunks, n_lat_chunks=n_lat_chunks, nb=nb),
        grid=(g,),
        in_specs=[
            pl.BlockSpec((1, rows, w), lambda i: (i, 0, 0)),
            pl.BlockSpec((1, w, w), lambda i: (i, 0, 0)),
            pl.BlockSpec((1, w, 4 * S5_STATE), lambda i: (i, 0, 0)),
            pl.BlockSpec((1, 4 * S5_STATE, w), lambda i: (i, 0, 0)),
            pl.BlockSpec((1, 2, 2 * S5_STATE), lambda i: (i, 0, 0)),
        ],
        out_specs=pl.BlockSpec((1, rows, w), lambda i: (i, 0, 0)),
        out_shape=jax.ShapeDtypeStruct((g, rows, w), F32),
        scratch_shapes=[pltpu.VMEM((rows, 4 * S5_STATE), F32), pltpu.VMEM((rows, 4 * S5_STATE), F32)],
        compiler_params=_cparams(("parallel",)),
        name="s5_core",
    )(xg, mt, pm, qm, a)


def _s5_weights(lam_re, lam_im, log_dt, b_re, b_im, c_re, c_im):
    t = S5_CHUNK
    lam = lax.complex(lam_re.astype(F32), lam_im.astype(F32))
    dt = jnp.exp(log_dt.astype(F32))[..., None]
    ldt = lam * dt
    lam_bar = jnp.exp(ldt)
    b_bar = ((lam_bar - 1) / lam)[..., None] * lax.complex(b_re.astype(F32), b_im.astype(F32))
    cm = lax.complex(c_re.astype(F32), c_im.astype(F32))
    tau = jnp.arange(t + 1, dtype=F32)
    pw = jnp.exp(ldt[:, None] * tau[None, :, None, None])
    hp = lax.Precision.HIGHEST
    taps = jnp.einsum('dgop,dtgp,dgpi->dtgoi', cm, pw[:, :t], b_bar, precision=hp).real
    tt = jnp.arange(t)
    lag = tt[:, None] - tt[None, :]
    kf = jnp.where((lag >= 0)[:, :, None, None, None], taps[0][jnp.clip(lag, 0, t - 1)], 0.0)
    kb = jnp.where((lag <= 0)[:, :, None, None, None], taps[1][jnp.clip(-lag, 0, t - 1)], 0.0)
    g = lam_re.shape[1]
    mt = (kf + kb).transpose(2, 1, 4, 0, 3).reshape(g, t * S5_GROUP, t * S5_GROUP)
    pf = pw[0, :t][::-1][:, :, :, None] * b_bar[0][None]
    pb = pw[1, :t][:, :, :, None] * b_bar[1][None]
    pm = jnp.concatenate([pf.real, pb.real, pf.imag, pb.imag], axis=2)
    pm = pm.transpose(1, 0, 3, 2).reshape(g, t * S5_GROUP, 4 * S5_STATE)
    qf = cm[0][None] * pw[0, 1:t + 1][:, :, None, :]
    qb = cm[1][None] * pw[1, 1:t + 1][::-1][:, :, None, :]
    qm = jnp.concatenate([qf.real, qb.real, -qf.imag, -qb.imag], axis=3)
    qm = qm.transpose(1, 3, 0, 2).reshape(g, 4 * S5_STATE, t * S5_GROUP)
    at = pw[:, t]
    a = jnp.stack([jnp.concatenate([at[0].real, at[1].real], axis=-1),
                   jnp.concatenate([at[0].imag, at[1].imag], axis=-1)], axis=1)
    return mt.astype(BF16), pm.astype(BF16), qm.astype(BF16), a.astype(F32)


def _gelu_tanh(x):
    return 0.5 * x * (1.0 + jnp.tanh(math.sqrt(2.0 / math.pi) * (x + 0.044715 * (x * x * x))))


def _glu_kernel(h_ref, y_ref, g_ref, mod_ref, d_ref, w_ref, b_ref, o_ref):
    h = h_ref[...]
    u = _norm_mod(h, g_ref[...], _mod_row(mod_ref, 0), _mod_row(mod_ref, 1))
    z = _gelu_tanh(y_ref[...] + d_ref[...] * u)
    acc = jnp.dot(z.astype(BF16), w_ref[...], preferred_element_type=F32) + b_ref[...]
    o_ref[...] = h + _mod_row(mod_ref, 2) * (z * _sigmoid(acc))


def _glu_call(h, y, g, mod, dvec, w_glu_b, b_glu, tpb, nb):
    n, d = h.shape
    return pl.pallas_call(
        _glu_kernel,
        grid=(n // TM,),
        in_specs=[_row_spec(d, tpb), _row_spec(d, tpb), _const_spec((1, d)), _mod_spec(d, tpb, nb),
                  _const_spec((1, d)), _const_spec((d, d)), _const_spec((1, d))],
        out_specs=_row_spec(d, tpb),
        out_shape=jax.ShapeDtypeStruct((n, d), F32),
        compiler_params=_cparams(("parallel",)),
        name="s5_glu",
    )(h, y, g.reshape(1, d), mod, dvec.reshape(1, d), w_glu_b, b_glu.reshape(1, d))


def _router_kernel(h_ref, g_ref, mod_ref, wr_ref, br_ref, ids_ref, gate_ref):
    v = _norm_mod(h_ref[...], g_ref[...], _mod_row(mod_ref, 3), _mod_row(mod_ref, 4))
    lg = jnp.dot(v, wr_ref[...], preferred_element_type=F32, precision=lax.Precision.HIGHEST) + br_ref[...]
    lane = lax.broadcasted_iota(jnp.int32, lg.shape, 1)
    neg = -jnp.inf
    big = jnp.int32(LANES)
    gl = jnp.where(lane < N_GROUPS, lg, neg)
    gmax = jnp.max(gl, axis=-1, keepdims=True)
    gidx = jnp.min(jnp.where(gl == gmax, lane, big), axis=-1, keepdims=True)
    g_w = 1.0 / jnp.sum(jnp.exp(gl - gmax), axis=-1, keepdims=True)
    lo = N_GROUPS + EXPERTS_PER_GROUP * gidx
    el = jnp.where((lane >= lo) & (lane < lo + EXPERTS_PER_GROUP), lg, neg)
    m1 = jnp.max(el, axis=-1, keepdims=True)
    i1 = jnp.min(jnp.where(el == m1, lane, big), axis=-1, keepdims=True)
    el2 = jnp.where(lane == i1, neg, el)
    m2 = jnp.max(el2, axis=-1, keepdims=True)
    i2 = jnp.min(jnp.where(el2 == m2, lane, big), axis=-1, keepdims=True)
    dd = jnp.exp(m2 - m1)
    w1 = g_w / (1.0 + dd)
    w2 = g_w * dd / (1.0 + dd)
    ids_ref[...] = jnp.where(lane == 0, i1 - N_GROUPS, jnp.where(lane == 1, i2 - N_GROUPS, 0))
    gate_ref[...] = jnp.where(lane == 0, w1, jnp.where(lane == 1, w2, 0.0))


def _router_call(h, g, mod, wr, br, tpb, nb):
    n, d = h.shape
    return pl.pallas_call(
        _router_kernel,
        grid=(n // TM,),
        in_specs=[_row_spec(d, tpb), _const_spec((1, d)), _mod_spec(d, tpb, nb),
                  _const_spec((d, LANES)), _const_spec((1, LANES))],
        out_specs=[_row_spec(LANES, tpb), _row_spec(LANES, tpb)],
        out_shape=[jax.ShapeDtypeStruct((n, LANES), jnp.int32), jax.ShapeDtypeStruct((n, LANES), F32)],
        compiler_params=_cparams(("parallel",)),
        name="moe_router",
    )(h, g.reshape(1, d), mod, wr, br)


def _row_copy(src, src_row, dst, dst_row, sem):
    return pltpu.make_async_copy(src.at[pl.ds(src_row, 1), :], dst.at[pl.ds(dst_row, 1), :], sem)


def _dispatch_kernel(slot_ref, h_ref, g_ref, mod_ref, xs_hbm, vbuf, sem):
    i = pl.program_id(0)
    last = pl.num_programs(0) - 1
    cur = i % 2
    vbuf[cur] = _norm_mod(h_ref[...], g_ref[...], _mod_row(mod_ref, 3), _mod_row(mod_ref, 4))

    def body(r, carry):
        for kk in range(TOP_K):
            _row_copy(vbuf.at[cur], r, xs_hbm, slot_ref[0, 0, kk * TM + r], sem.at[cur]).start()
        return carry
    lax.fori_loop(0, TM, body, 0, unroll=8)

    def wait_all(b):
        for _ in range(TOP_K):
            pltpu.make_async_copy(vbuf.at[b], xs_hbm.at[pl.ds(0, TM), :], sem.at[b]).wait()

    @pl.when(i > 0)
    def _():
        wait_all(1 - cur)

    @pl.when(i == last)
    def _():
        wait_all(cur)


def _dispatch_call(h, g, mod, slots_kt, tpb, nb):
    n, d = h.shape
    return pl.pallas_call(
        _dispatch_kernel,
        grid=(n // TM,),
        in_specs=[pl.BlockSpec((1, 1, TOP_K * TM), lambda i: (i, 0, 0), memory_space=pltpu.SMEM),
                  _row_spec(d, tpb), _const_spec((1, d)), _mod_spec(d, tpb, nb)],
        out_specs=pl.BlockSpec(memory_space=pl.ANY),
        out_shape=jax.ShapeDtypeStruct((TOP_K * n, d), F32),
        scratch_shapes=[pltpu.VMEM((2, TM, d), F32), pltpu.SemaphoreType.DMA((2,))],
        compiler_params=_cparams(("arbitrary",)),
        name="moe_dispatch",
    )(slots_kt, h, g.reshape(1, d), mod)


def _ffn_kernel(vb_ref, ve_ref, vlo_ref, vhi_ref, nv_ref, x_ref, w1_ref, w3_ref, w2_ref, y_ref, w1b, w3b, w2b):
    v = pl.program_id(0)
    live = v < nv_ref[0]
    prev = jnp.maximum(v - 1, 0)
    new_expert = (v == 0) | (ve_ref[v] != ve_ref[prev])
    new_block = (v == 0) | (vb_ref[v] != vb_ref[prev])

    @pl.when(new_expert & live)
    def _():
        w1b[...] = w1_ref[0].astype(BF16)
        w3b[...] = w3_ref[0].astype(BF16)
        w2b[...] = w2_ref[0].astype(BF16)

    @pl.when(live)
    def _():
        x = x_ref[...].astype(BF16)
        h1 = jnp.dot(x, w1b[...], preferred_element_type=F32)
        h3 = jnp.dot(x, w3b[...], preferred_element_type=F32)
        act = (h1 * _sigmoid(h1)) * h3
        y = jnp.dot(act.astype(BF16), w2b[...], preferred_element_type=F32)
        row = lax.broadcasted_iota(jnp.int32, y.shape, 0)
        mine = (row >= vlo_ref[v]) & (row < vhi_ref[v])
        y_ref[...] = jnp.where(mine, y, jnp.where(new_block, 0.0, y_ref[...]))


def _ffn_call(xs, vis, w1, w3, w2):
    n2, d = xs.shape
    ne, _, de = w1.shape
    n_vis = vis[0].shape[0]
    grid_spec = pltpu.PrefetchScalarGridSpec(
        num_scalar_prefetch=5,
        grid=(n_vis,),
        in_specs=[
            pl.BlockSpec((MOE_TILE, d), lambda v, vb, ve, lo, hi, nv: (vb[v], 0)),
            pl.BlockSpec((1, d, de), lambda v, vb, ve, lo, hi, nv: (ve[v], 0, 0)),
            pl.BlockSpec((1, d, de), lambda v, vb, ve, lo, hi, nv: (ve[v], 0, 0)),
            pl.BlockSpec((1, de, d), lambda v, vb, ve, lo, hi, nv: (ve[v], 0, 0)),
        ],
        out_specs=pl.BlockSpec((MOE_TILE, d), lambda v, vb, ve, lo, hi, nv: (vb[v], 0)),
        scratch_shapes=[pltpu.VMEM((d, de), BF16), pltpu.VMEM((d, de), BF16), pltpu.VMEM((de, d), BF16)],
    )
    return pl.pallas_call(
        _ffn_kernel,
        grid_spec=grid_spec,
        out_shape=jax.ShapeDtypeStruct((n2, d), F32),
        compiler_params=_cparams(("arbitrary",)),
        name="moe_ffn",
    )(*vis, xs, w1, w3, w2)


def _combine_kernel(s0_ref, sn_ref, yb_hbm, h_ref, gate_ref, mod_ref, o_ref, ybuf, sem):
    i = pl.program_id(0)
    n_steps = pl.num_programs(0)
    cur = i % 2

    def gather(idx_ref, b):
        def body(r, carry):
            _row_copy(yb_hbm, idx_ref[0, 0, r], ybuf.at[b], r, sem.at[b]).start()
            return carry
        lax.fori_loop(0, TOP_K * TM, body, 0, unroll=8)

    @pl.when(i == 0)
    def _():
        gather(s0_ref, 0)

    @pl.when(i + 1 < n_steps)
    def _():
        gather(sn_ref, 1 - cur)

    pltpu.make_async_copy(yb_hbm.at[pl.ds(0, TOP_K * TM), :], ybuf.at[cur], sem.at[cur]).wait()
    gate = gate_ref[...]
    y = gate[:, 0:1] * ybuf[cur, 0:TM, :] + gate[:, 1:2] * ybuf[cur, TM:2 * TM, :]
    o_ref[...] = h_ref[...] + _mod_row(mod_ref, 5) * y


def _combine_call(h, yb, slots_kt, gates, mod, tpb, nb):
    n, d = h.shape
    n_tiles = n // TM
    return pl.pallas_call(
        _combine_kernel,
        grid=(n_tiles,),
        in_specs=[
            pl.BlockSpec((1, 1, TOP_K * TM), lambda i: (0, 0, 0), memory_space=pltpu.SMEM),
            pl.BlockSpec((1, 1, TOP_K * TM), lambda i: (jnp.minimum(i + 1, n_tiles - 1), 0, 0),
                         memory_space=pltpu.SMEM),
            pl.BlockSpec(memory_space=pl.ANY),
            _row_spec(d, tpb), _row_spec(LANES, tpb), _mod_spec(d, tpb, nb),
        ],
        out_specs=_row_spec(d, tpb),
        out_shape=jax.ShapeDtypeStruct((n, d), F32),
        scratch_shapes=[pltpu.VMEM((2, TOP_K * TM, d), F32), pltpu.SemaphoreType.DMA((2,))],
        compiler_params=_cparams(("arbitrary",)),
        name="moe_combine",
    )(slots_kt, slots_kt, yb, h, gates, mod)


def _routing_tables(ids, n_tok):
    n = n_tok * TOP_K
    expert = ids[:, :TOP_K].reshape(n)
    e_iota = jnp.arange(N_EXPERTS, dtype=jnp.int32)
    onehot = expert[:, None] == e_iota[None, :]
    csum = jnp.cumsum(onehot.astype(jnp.int32), axis=0)
    counts = csum[-1]
    send = jnp.cumsum(counts)
    sstart = send - counts
    rank = jnp.sum(jnp.where(onehot, csum, 0), axis=1) - 1
    row_of = jnp.sum(jnp.where(onehot, sstart[None, :], 0), axis=1) + rank
    n_tiles = n_tok // TM
    slots_kt = row_of.reshape(n_tiles, TM, TOP_K).transpose(0, 2, 1).reshape(n_tiles, 1, TOP_K * TM)

    first_blk = sstart // MOE_TILE
    last_blk = jnp.maximum(send - 1, 0) // MOE_TILE
    nvis = jnp.where(counts > 0, last_blk - first_blk + 1, 0)
    vend = jnp.cumsum(nvis)
    vstart = vend - nvis
    n_vis_max = n // MOE_TILE + N_EXPERTS
    vi = jnp.arange(n_vis_max, dtype=jnp.int32)
    total = vend[-1]
    vc = jnp.minimum(vi, total - 1)
    ve = jnp.sum(vc[:, None] >= vend[None, :], axis=1).astype(jnp.int32)
    sel = ve[:, None] == e_iota[None, :]
    pick = lambda t: jnp.sum(jnp.where(sel, t[None, :], 0), axis=1)
    vb = pick(first_blk) + vc - pick(vstart)
    lo = jnp.maximum(pick(sstart), vb * MOE_TILE) - vb * MOE_TILE
    hi = jnp.minimum(pick(send), (vb + 1) * MOE_TILE) - vb * MOE_TILE
    vis = tuple(t.astype(jnp.int32) for t in (vb, ve, lo, hi, total.reshape(1)))
    return slots_kt.astype(jnp.int32), vis


def _moe_layer(h, g2n, mod, w_grp, b_grp, w_rt, b_rt, w1, w3, w2, tpb, nb):
    n, d = h.shape
    pad = LANES - N_GROUPS - N_EXPERTS
    wr = jnp.concatenate([w_grp, w_rt, jnp.zeros((d, pad), F32)], axis=1)
    br = jnp.concatenate([b_grp, b_rt, jnp.zeros((pad,), F32)]).reshape(1, LANES)
    ids, gates = _router_call(h, g2n, mod, wr, br, tpb, nb)
    slots_kt, vis = _routing_tables(ids, n)
    xs = _dispatch_call(h, g2n, mod, slots_kt, tpb, nb)
    yb = _ffn_call(xs, vis, w1, w3, w2)
    return _combine_call(h, yb, slots_kt, gates, mod, tpb, nb)


def _s5_layer(h, g1n, mod, prm, tpb, nb):
    n, d = h.shape
    nt = n // nb
    n_chunks = nt // S5_CHUNK
    groups = d // S5_GROUP
    u = _norm_mod_call(h, g1n, mod, tpb, nb)
    xg = u.reshape(nb, n_chunks, S5_CHUNK, groups, S5_GROUP).transpose(3, 1, 0, 2, 4)
    xg = xg.reshape(groups, n_chunks * nb, S5_CHUNK * S5_GROUP)
    mt, pm, qm, a = _s5_weights(prm['lam_re'], prm['lam_im'], prm['log_dt'], prm['b_re'], prm['b_im'],
                                prm['c_re'], prm['c_im'])
    n_lat_chunks = (nt - prm['ctx_len']) // S5_CHUNK
    yg = _s5_core_call(xg, mt, pm, qm, a, n_chunks, n_lat_chunks, nb)
    y = yg.reshape(groups, n_chunks, nb, S5_CHUNK, S5_GROUP).transpose(2, 1, 3, 0, 4).reshape(n, d)
    return _glu_call(h, y, g1n, mod, prm['d'], prm['w_glu'].astype(BF16), prm['b_glu'], tpb, nb)


def _rope_tables(seq, ctx_len):
    rows = seq // GRID_W
    t_row = jnp.repeat(jnp.arange(rows), GRID_W).astype(F32)
    t_col = jnp.tile(jnp.arange(GRID_W), rows).astype(F32)
    half = HEAD_DIM // 2
    inv = ROPE_BASE ** (-jnp.arange(0, half, 2, dtype=F32) / half)
    ar = t_row[:, None] * inv
    ac = t_col[:, None] * inv
    ang = jnp.concatenate([ar, ar, ac, ac], axis=-1)
    cos = jnp.cos(ang)
    sin = jnp.sin(ang)
    lane = jnp.arange(HEAD_DIM)
    sin = jnp.where((lane & 32) == 0, -sin, sin)
    cos = jnp.concatenate([cos, jnp.ones((ctx_len, HEAD_DIM), F32)], axis=0)
    sin = jnp.concatenate([sin, jnp.zeros((ctx_len, HEAD_DIM), F32)], axis=0)
    return cos, sin


def kernel(x, c, ctx, c_ctx, w_mod, b_mod, norm1_g, norm2_g, s5_lam_re, s5_lam_im, s5_log_dt, s5_b_re, s5_b_im, s5_c_re, s5_c_im, s5_d, s5_w_glu, s5_b_glu, swa_w_qkv, swa_q_g, swa_k_g, swa_sink, swa_w_o, ga_w_qkv, ga_q_g, ga_k_g, ga_w_o, moe_w_grp, moe_b_grp, moe_w_rt, moe_b_rt, moe_w1, moe_w3, moe_w2):
    nb, seq, d = x.shape
    ctx_len = ctx.shape[1]
    depth = w_mod.shape[0]
    nt = seq + ctx_len
    n = nb * nt
    tpb = nt // TM
    assert nt % TM == 0 and ctx_len == TM and nb < 8 and nt % S5_CHUNK == 0 and (TOP_K * n) % MOE_TILE == 0

    c8 = jnp.zeros((8, d), F32).at[:nb].set(c).at[nb].set(c_ctx)
    mods = _ada_mod_all(c8, w_mod, b_mod).reshape(depth, 8, N_MOD, d)
    cos, sin = _rope_tables(seq, ctx_len)

    h = jnp.concatenate([x, ctx], axis=1).reshape(n, d)
    for i in range(depth):
        kind, j = i % 3, i // 3
        mod = mods[i]
        if kind == 0:
            prm = dict(lam_re=s5_lam_re[j], lam_im=s5_lam_im[j], log_dt=s5_log_dt[j], b_re=s5_b_re[j],
                       b_im=s5_b_im[j], c_re=s5_c_re[j], c_im=s5_c_im[j], d=s5_d[j], w_glu=s5_w_glu[j],
                       b_glu=s5_b_glu[j], ctx_len=ctx_len)
            h = _s5_layer(h, norm1_g[i], mod, prm, tpb, nb)
        elif kind == 1:
            q, k, v = _qkv_call(h, norm1_g[i], mod, cos, sin, swa_q_g[j], swa_k_g[j],
                                swa_w_qkv[j].astype(BF16), tpb, nb)
            o = _swa_call(q, k, v, swa_sink[j], nb, seq, ctx_len)
            h = _oproj_call(o, swa_w_o[j].astype(BF16), h, mod, tpb, nb)
        else:
            q, k, v = _qkv_call(h, norm1_g[i], mod, cos, sin, ga_q_g[j], ga_k_g[j],
                                ga_w_qkv[j].astype(BF16), tpb, nb)
            o = _ga_call(q, k, v, nb, seq, ctx_len)
            h = _oproj_call(o, ga_w_o[j].astype(BF16), h, mod, tpb, nb)
        h = _moe_layer(h, norm2_g[i], mod, moe_w_grp[i], moe_b_grp[i], moe_w_rt[i], moe_b_rt[i],
                       moe_w1[i], moe_w3[i], moe_w2[i], tpb, nb)
    return h.reshape(nb, nt, d)[:, :seq]
```

```python
import functools
import math

import jax
import jax.numpy as jnp
from jax import lax
from jax.experimental import pallas as pl
from jax.experimental.pallas import tpu as pltpu

F32 = jnp.float32
BF16 = jnp.bfloat16

EPS = 1e-6
N_MOD = 6
HEAD_DIM = 128
N_HEADS = 16
N_KV_HEADS = 4
REP = N_HEADS // N_KV_HEADS
WINDOW = 128
GRID_W = 64
ROPE_BASE = 10000.0
S5_GROUP = 16
S5_STATE = 64
N_GROUPS = 4
EXPERTS_PER_GROUP = 8
N_EXPERTS = N_GROUPS * EXPERTS_PER_GROUP
TOP_K = 2
ATTN_SCALE = HEAD_DIM ** -0.5

LANES = 128
TM = 256
S5_CHUNK = 16
MOE_TILE = 256
VMEM_LIMIT = 56 * 1024 * 1024


def _cparams(sem, vmem=VMEM_LIMIT):
    return pltpu.CompilerParams(dimension_semantics=sem, vmem_limit_bytes=vmem)


def _sigmoid(x):
    return 1.0 / (1.0 + jnp.exp(-x))


def _norm_mod(x, g, shift, scale):
    ms = jnp.mean(x * x, axis=-1, keepdims=True)
    y = x * lax.rsqrt(ms + EPS) * g
    return y * (1.0 + scale) + shift


def _mod_row(mod_ref, k):
    return mod_ref[0, k:k + 1, :]


def _mod_kernel(c_ref, w_ref, b_ref, o_ref):
    c = c_ref[...]
    cond = c * _sigmoid(c)
    acc = jnp.dot(cond.astype(BF16), w_ref[...].astype(BF16), preferred_element_type=F32)
    o_ref[...] = acc + b_ref[...]


def _ada_mod_all(c8, w_mod, b_mod):
    depth, d, n6 = w_mod.shape
    tn = 1536
    return pl.pallas_call(
        _mod_kernel,
        grid=(depth, n6 // tn),
        in_specs=[
            pl.BlockSpec((8, d), lambda l, j: (0, 0)),
            pl.BlockSpec((None, d, tn), lambda l, j: (l, 0, j)),
            pl.BlockSpec((None, 1, tn), lambda l, j: (l, 0, j)),
        ],
        out_specs=pl.BlockSpec((None, 8, tn), lambda l, j: (l, 0, j)),
        out_shape=jax.ShapeDtypeStruct((depth, 8, n6), F32),
        compiler_params=_cparams(("parallel", "parallel")),
        name="ada_mod",
    )(c8, w_mod, b_mod.reshape(depth, 1, n6))


def _row_spec(width, tpb):
    del tpb
    return pl.BlockSpec((TM, width), lambda i: (i, 0))


def _mod_spec(d, tpb, nb):
    return pl.BlockSpec((1, N_MOD, d), lambda i: (jnp.where(i % tpb == tpb - 1, nb, i // tpb), 0, 0))


def _const_spec(shape):
    nd = len(shape)
    return pl.BlockSpec(shape, lambda i: (0,) * nd)


S5_SUB = 8
TILE_CHUNKS = TM // S5_CHUNK


def _granule(shape):
    return lax.broadcasted_iota(jnp.int32, shape, 1) // S5_GROUP


def _s5_in_kernel(h_ref, g_ref, mod_ref, x_ref, u_scr):
    u = _norm_mod(h_ref[...], g_ref[...], _mod_row(mod_ref, 0), _mod_row(mod_ref, 1))
    n_lt = u.shape[1] // LANES
    for lt in range(n_lt):
        u_scr[lt] = u[:, lt * LANES:(lt + 1) * LANES]
    gran = _granule((TILE_CHUNKS, LANES))
    for lt in range(n_lt):
        for eta in range(S5_CHUNK // S5_SUB):
            rolled = []
            for dl in range(S5_SUB):
                r = u_scr[lt, pl.ds(S5_SUB * eta + dl, TILE_CHUNKS, stride=S5_CHUNK), :]
                rolled.append(pltpu.roll(r, dl * S5_GROUP, 1) if dl else r)
            for gl in range(S5_SUB):
                o = rolled[0]
                for dl in range(1, S5_SUB):
                    o = jnp.where(gran == (gl + dl) % S5_SUB, rolled[dl], o)
                x_ref[lt * S5_SUB + gl, :, eta * LANES:(eta + 1) * LANES] = o.astype(x_ref.dtype)


def _s5_in_call(h, g, mod, tpb, nb):
    n, d = h.shape
    groups = d // S5_GROUP
    return pl.pallas_call(
        _s5_in_kernel,
        grid=(n // TM,),
        in_specs=[_row_spec(d, tpb), _const_spec((1, d)), _mod_spec(d, tpb, nb)],
        out_specs=pl.BlockSpec((groups, TILE_CHUNKS, S5_CHUNK * S5_GROUP), lambda i: (0, i, 0)),
        out_shape=jax.ShapeDtypeStruct((groups, n // S5_CHUNK, S5_CHUNK * S5_GROUP), BF16),
        scratch_shapes=[pltpu.VMEM((d // LANES, TM, LANES), F32)],
        compiler_params=_cparams(("parallel",)),
        name="s5_in",
    )(h, g.reshape(1, d), mod)


def _qkv_kernel(h_ref, g_ref, mod_ref, cos_ref, sin_ref, qg_ref, kg_ref, w_ref, q_ref, k_ref, v_ref):
    u = _norm_mod(h_ref[...], g_ref[...], _mod_row(mod_ref, 0), _mod_row(mod_ref, 1))
    p = jnp.dot(u.astype(BF16), w_ref[...], preferred_element_type=F32)
    cos = cos_ref[...]
    sin = sin_ref[...]
    lane = lax.broadcasted_iota(jnp.int32, (TM, HEAD_DIM), 1)
    first = (lane & 32) == 0

    def head(xh, gain, scale):
        ms = jnp.mean(xh * xh, axis=-1, keepdims=True)
        xn = xh * lax.rsqrt(ms + EPS) * gain
        rot = jnp.where(first, pltpu.roll(xn, HEAD_DIM - 32, 1), pltpu.roll(xn, 32, 1))
        return (xn * cos + rot * sin) * scale

    qg = qg_ref[...]
    kg = kg_ref[...]
    nq = N_HEADS * HEAD_DIM
    nk = N_KV_HEADS * HEAD_DIM
    for hh in range(N_HEADS):
        sl = slice(hh * HEAD_DIM, (hh + 1) * HEAD_DIM)
        q_ref[:, sl] = head(p[:, sl], qg, ATTN_SCALE).astype(BF16)
    for hh in range(N_KV_HEADS):
        sl = slice(hh * HEAD_DIM, (hh + 1) * HEAD_DIM)
        k_ref[:, sl] = head(p[:, nq + hh * HEAD_DIM:nq + (hh + 1) * HEAD_DIM], kg, 1.0).astype(BF16)
    v_ref[...] = p[:, nq + nk:].astype(BF16)


def _qkv_call(h, g, mod, cos, sin, q_g, k_g, w_qkv_b, tpb, nb):
    n, d = h.shape
    nq = N_HEADS * HEAD_DIM
    nk = N_KV_HEADS * HEAD_DIM
    return pl.pallas_call(
        _qkv_kernel,
        grid=(n // TM,),
        in_specs=[
            _row_spec(d, tpb), _const_spec((1, d)), _mod_spec(d, tpb, nb),
            pl.BlockSpec((TM, HEAD_DIM), lambda i: (i % tpb, 0)),
            pl.BlockSpec((TM, HEAD_DIM), lambda i: (i % tpb, 0)),
            _const_spec((1, HEAD_DIM)), _const_spec((1, HEAD_DIM)),
            _const_spec((d, nq + 2 * nk)),
        ],
        out_specs=[_row_spec(nq, tpb), _row_spec(nk, tpb), _row_spec(nk, tpb)],
        out_shape=[jax.ShapeDtypeStruct((n, nq), BF16), jax.ShapeDtypeStruct((n, nk), BF16),
                   jax.ShapeDtypeStruct((n, nk), BF16)],
        compiler_params=_cparams(("parallel",)),
        name="qkv_proj",
    )(h, g.reshape(1, d), mod, cos, sin, q_g.reshape(1, HEAD_DIM), k_g.reshape(1, HEAD_DIM), w_qkv_b)


def _oproj_kernel(a_ref, w_ref, h_ref, mod_ref, o_ref):
    acc = jnp.dot(a_ref[...], w_ref[...], preferred_element_type=F32)
    o_ref[...] = h_ref[...] + _mod_row(mod_ref, 2) * acc


def _oproj_call(a, w_o_b, h, mod, tpb, nb):
    n, d = h.shape
    return pl.pallas_call(
        _oproj_kernel,
        grid=(n // TM,),
        in_specs=[_row_spec(a.shape[1], tpb), _const_spec(w_o_b.shape), _row_spec(d, tpb), _mod_spec(d, tpb, nb)],
        out_specs=_row_spec(d, tpb),
        out_shape=jax.ShapeDtypeStruct((n, d), F32),
        compiler_params=_cparams(("parallel",)),
        name="o_proj",
    )(a, w_o_b, h, mod)


GA_TQ = 256
GA_TK = 512
SWA_TQ = 256


def _stack_heads(q_ref):
    return jnp.concatenate([q_ref[:, r * HEAD_DIM:(r + 1) * HEAD_DIM] for r in range(REP)], axis=0)


def _unstack_heads(o, o_ref):
    tq = o_ref.shape[0]
    for r in range(REP):
        o_ref[:, r * HEAD_DIM:(r + 1) * HEAD_DIM] = o[r * tq:(r + 1) * tq, :].astype(o_ref.dtype)


def _with_ones(v):
    return jnp.concatenate([v, jnp.ones_like(v)], axis=1)


def _flash_step(q, k, v, m, acc):
    s = lax.dot_general(q, k, (((1,), (1,)), ((), ())), preferred_element_type=F32)
    m_new = jnp.maximum(m, jnp.max(s, axis=-1, keepdims=True))
    alpha = jnp.exp(m - m_new)
    p = jnp.exp(s - m_new[:, :1]).astype(BF16)
    pv = jnp.dot(p, _with_ones(v), preferred_element_type=F32)
    return m_new, jnp.concatenate([alpha, alpha], axis=1) * acc + pv


def _ga_kernel(q_ref, k_ref, v_ref, o_ref, *, n_lat_tiles, seq):
    qt = pl.program_id(2)
    q = _stack_heads(q_ref)
    rows = q.shape[0]
    m0 = jnp.full((rows, HEAD_DIM), -jnp.inf, F32)
    a0 = jnp.zeros((rows, 2 * HEAD_DIM), F32)

    def finish(acc):
        _unstack_heads(acc[:, :HEAD_DIM] / acc[:, HEAD_DIM:], o_ref)

    @pl.when(qt < n_lat_tiles)
    def _():
        m, acc = m0, a0
        for c in range(seq // GA_TK):
            m, acc = _flash_step(q, k_ref[c * GA_TK:(c + 1) * GA_TK, :], v_ref[c * GA_TK:(c + 1) * GA_TK, :], m, acc)
        m, acc = _flash_step(q, k_ref[seq:, :], v_ref[seq:, :], m, acc)
        finish(acc)

    @pl.when(qt >= n_lat_tiles)
    def _():
        finish(_flash_step(q, k_ref[seq:, :], v_ref[seq:, :], m0, a0)[1])


def _attn_specs(tq, nt):
    n_qt = nt // tq
    q_spec = pl.BlockSpec((tq, REP * HEAD_DIM), lambda b, g, i: (b * n_qt + i, g))
    kv_spec = pl.BlockSpec((nt, HEAD_DIM), lambda b, g, i: (b, g))
    return n_qt, q_spec, kv_spec


def _ga_call(q, k, v, nb, seq, ctx_len):
    n = q.shape[0]
    nt = seq + ctx_len
    assert seq % GA_TK == 0 and seq % GA_TQ == 0 and ctx_len % GA_TQ == 0
    n_qt, q_spec, kv_spec = _attn_specs(GA_TQ, nt)
    return pl.pallas_call(
        functools.partial(_ga_kernel, n_lat_tiles=seq // GA_TQ, seq=seq),
        grid=(nb, N_KV_HEADS, n_qt),
        in_specs=[q_spec, kv_spec, kv_spec],
        out_specs=q_spec,
        out_shape=jax.ShapeDtypeStruct((n, N_HEADS * HEAD_DIM), BF16),
        compiler_params=_cparams(("parallel", "parallel", "parallel")),
        name="ga_attn",
    )(q, k, v)


def _swa_kernel(sink_ref, q_ref, k_ref, v_ref, o_ref, *, n_lat_tiles, seq):
    g = pl.program_id(1)
    i = pl.program_id(2)
    tq = q_ref.shape[0]
    nwin = tq + 2 * WINDOW
    q = _stack_heads(q_ref)
    rows = q.shape[0]
    start = pl.multiple_of(jnp.clip(i * tq - WINDOW, 0, seq - nwin), WINDOW)
    k_all = jnp.concatenate([k_ref[pl.ds(start, nwin), :], k_ref[seq:, :]], axis=0)
    v_all = jnp.concatenate([v_ref[pl.ds(start, nwin), :], v_ref[seq:, :]], axis=0)
    nk = k_all.shape[0]
    col = lax.broadcasted_iota(jnp.int32, (rows, nk), 1)
    row = lax.broadcasted_iota(jnp.int32, (rows, nk), 0)
    qbase = jnp.where(i < n_lat_tiles, i * tq, -(1 << 20))
    qpos = qbase + (row & (tq - 1))
    mask = (jnp.abs(qpos - (start + col)) <= WINDOW) | (col >= nwin)
    s = lax.dot_general(q, k_all, (((1,), (1,)), ((), ())), preferred_element_type=F32)
    s = jnp.where(mask, s, -jnp.inf)
    head = lax.broadcasted_iota(jnp.int32, (rows, HEAD_DIM), 0) // tq
    sk = jnp.zeros((rows, HEAD_DIM), F32)
    for r in range(REP):
        sk = jnp.where(head == r, sink_ref[g * REP + r], sk)
    m = jnp.maximum(jnp.max(s, axis=-1, keepdims=True), sk)
    p = jnp.exp(s - m[:, :1]).astype(BF16)
    acc = jnp.dot(p, _with_ones(v_all), preferred_element_type=F32)
    _unstack_heads(acc[:, :HEAD_DIM] / (acc[:, HEAD_DIM:] + jnp.exp(sk - m)), o_ref)


def _swa_call(q, k, v, sink, nb, seq, ctx_len):
    n = q.shape[0]
    nt = seq + ctx_len
    assert seq % SWA_TQ == 0 and ctx_len % SWA_TQ == 0 and SWA_TQ % WINDOW == 0 and SWA_TQ & (SWA_TQ - 1) == 0
    n_qt, q_spec, kv_spec = _attn_specs(SWA_TQ, nt)
    return pl.pallas_call(
        functools.partial(_swa_kernel, n_lat_tiles=seq // SWA_TQ, seq=seq),
        grid=(nb, N_KV_HEADS, n_qt),
        in_specs=[pl.BlockSpec(memory_space=pltpu.SMEM), q_spec, kv_spec, kv_spec],
        out_specs=q_spec,
        out_shape=jax.ShapeDtypeStruct((n, N_HEADS * HEAD_DIM), BF16),
        compiler_params=_cparams(("parallel", "parallel", "parallel")),
        name="swa_attn",
    )(sink, q, k, v)


def _s5_kernel(x_ref, mt_ref, p_ref, q_ref, a_ref, y_ref, s_scr, h_scr, *, n_chunks, n_lat_chunks, nb):
    half = S5_STATE
    for b in range(nb):
        sb = jnp.dot(x_ref[0, b * n_chunks:(b + 1) * n_chunks, :], p_ref[0], preferred_element_type=F32)
        s_scr[0, pl.ds(b, n_chunks, stride=nb), :] = sb[:, :2 * half]
        s_scr[1, pl.ds(b, n_chunks, stride=nb), :] = sb[:, 2 * half:]
    ar = jnp.broadcast_to(a_ref[0, 0:1, :], (nb, 2 * half))
    ai = jnp.broadcast_to(a_ref[0, 1:2, :], (nb, 2 * half))
    is_fwd = lax.broadcasted_iota(jnp.int32, (nb, 2 * half), 1) < half
    hr = jnp.zeros((nb, 2 * half), F32)
    hi = jnp.zeros((nb, 2 * half), F32)
    for step in range(n_chunks):
        rf = nb * ((step + n_lat_chunks) % n_chunks)
        rb = nb * (n_chunks - 1 - step)
        h_scr[0, rf:rf + nb, 0:half] = hr[:, 0:half]
        h_scr[1, rf:rf + nb, 0:half] = hi[:, 0:half]
        h_scr[0, rb:rb + nb, half:] = hr[:, half:]
        h_scr[1, rb:rb + nb, half:] = hi[:, half:]
        sr = jnp.where(is_fwd, s_scr[0, rf:rf + nb, :], s_scr[0, rb:rb + nb, :])
        si = jnp.where(is_fwd, s_scr[1, rf:rf + nb, :], s_scr[1, rb:rb + nb, :])
        hr, hi = ar * hr - ai * hi + sr, ar * hi + ai * hr + si
    for b in range(nb):
        hin = jnp.concatenate([h_scr[0, pl.ds(b, n_chunks, stride=nb), :],
                               h_scr[1, pl.ds(b, n_chunks, stride=nb), :]], axis=1).astype(BF16)
        y = jnp.dot(x_ref[0, b * n_chunks:(b + 1) * n_chunks, :], mt_ref[0], preferred_element_type=F32)
        y_ref[0, b * n_chunks:(b + 1) * n_chunks, :] = y + jnp.dot(hin, q_ref[0], preferred_element_type=F32)


def _s5_core_call(xg, mt, pm, qm, a, n_chunks, n_lat_chunks, nb):
    g, rows, w = xg.shape
    return pl.pallas_call(
        functools.partial(_s5_kernel, n_chunks=n_chunks, n_lat_chunks=n_lat_chunks, nb=nb),
        grid=(g,),
        in_specs=[
            pl.BlockSpec((1, rows, w), lambda i: (i, 0, 0)),
            pl.BlockSpec((1, w, w), lambda i: (i, 0, 0)),
            pl.BlockSpec((1, w, 4 * S5_STATE), lambda i: (i, 0, 0)),
            pl.BlockSpec((1, 4 * S5_STATE, w), lambda i: (i, 0, 0)),
            pl.BlockSpec((1, 2, 2 * S5_STATE), lambda i: (i, 0, 0)),
        ],
        out_specs=pl.BlockSpec((1, rows, w), lambda i: (i, 0, 0)),
        out_shape=jax.ShapeDtypeStruct((g, rows, w), F32),
        scratch_shapes=[pltpu.VMEM((2, rows, 2 * S5_STATE), F32), pltpu.VMEM((2, rows, 2 * S5_STATE), F32)],
        compiler_params=_cparams(("parallel",)),
        name="s5_core",
    )(xg, mt, pm, qm, a)


def _s5_weights(lam_re, lam_im, log_dt, b_re, b_im, c_re, c_im):
    t = S5_CHUNK
    hp = lax.Precision.HIGHEST
    lam = lax.complex(lam_re.astype(F32), lam_im.astype(F32))
    dt = jnp.exp(log_dt.astype(F32))[..., None]
    ldt = lam * dt
    lam_bar = jnp.exp(ldt)
    b_bar = ((lam_bar - 1) / lam)[..., None] * lax.complex(b_re.astype(F32), b_im.astype(F32))
    cm = lax.complex(c_re.astype(F32), c_im.astype(F32))
    g = lam_re.shape[1]
    gt = g // S5_SUB
    tau = jnp.arange(t + 1, dtype=F32)
    pw = jnp.exp(ldt[:, None] * tau[None, :, None, None])
    taps = jnp.einsum('dgop,dtgp,dgpi->dtgoi', cm, pw[:, :t], b_bar, precision=hp).real
    pos = jnp.arange(t)
    tpos = S5_SUB * (pos[None, :] // S5_SUB) + (pos[None, :] % S5_SUB - jnp.arange(S5_SUB)[:, None]) % S5_SUB
    lag = tpos[:, :, None] - tpos[:, None, :]
    k_iota = jnp.arange(t)
    oh_f = (lag[..., None] == k_iota).astype(F32)
    oh_b = (-lag[..., None] == k_iota).astype(F32)
    taps_r = taps.reshape(2, t, gt, S5_SUB, S5_GROUP, S5_GROUP)
    kfb = (jnp.einsum('rpqk,kgroi->rpqgoi', oh_f, taps_r[0], precision=hp)
           + jnp.einsum('rpqk,kgroi->rpqgoi', oh_b, taps_r[1], precision=hp))
    mt = kfb.transpose(3, 0, 2, 5, 1, 4).reshape(g, t * S5_GROUP, t * S5_GROUP)
    oh_t = (tpos[..., None] == k_iota).astype(F32)
    pf = pw[0, :t][::-1][:, :, :, None] * b_bar[0][None]
    pb = pw[1, :t][:, :, :, None] * b_bar[1][None]
    pm = jnp.concatenate([pf.real, pb.real, pf.imag, pb.imag], axis=2)
    pm = jnp.einsum('rqs,sgrpi->grqip', oh_t, pm.reshape(t, gt, S5_SUB, 4 * S5_STATE, S5_GROUP), precision=hp)
    pm = pm.reshape(g, t * S5_GROUP, 4 * S5_STATE)
    qf = cm[0][None] * pw[0, 1:t + 1][:, :, None, :]
    qb = cm[1][None] * pw[1, 1:t + 1][::-1][:, :, None, :]
    qm = jnp.concatenate([qf.real, qb.real, -qf.imag, -qb.imag], axis=3)
    qm = jnp.einsum('rqs,sgrop->grpqo', oh_t, qm.reshape(t, gt, S5_SUB, S5_GROUP, 4 * S5_STATE), precision=hp)
    qm = qm.reshape(g, 4 * S5_STATE, t * S5_GROUP)
    at = pw[:, t]
    a = jnp.stack([jnp.concatenate([at[0].real, at[1].real], axis=-1),
                   jnp.concatenate([at[0].imag, at[1].imag], axis=-1)], axis=1)
    return mt.astype(BF16), pm.astype(BF16), qm.astype(BF16), a.astype(F32)


def _gelu_tanh(x):
    return 0.5 * x * (1.0 + jnp.tanh(math.sqrt(2.0 / math.pi) * (x + 0.044715 * (x * x * x))))


def _glu_kernel(h_ref, y_ref, g_ref, mod_ref, d_ref, w_ref, b_ref, o_ref, y_scr):
    h = h_ref[...]
    n_lt = h.shape[1] // LANES
    gran = _granule((TILE_CHUNKS, LANES))
    for lt in range(n_lt):
        for eta in range(S5_CHUNK // S5_SUB):
            ys = [y_ref[lt * S5_SUB + gl, :, eta * LANES:(eta + 1) * LANES] for gl in range(S5_SUB)]
            for dl in range(S5_SUB):
                o = ys[0]
                for gl in range(1, S5_SUB):
                    o = jnp.where(gran == (gl + dl) % S5_SUB, ys[gl], o)
                if dl:
                    o = pltpu.roll(o, LANES - dl * S5_GROUP, 1)
                y_scr[lt, pl.ds(S5_SUB * eta + dl, TILE_CHUNKS, stride=S5_CHUNK), :] = o
    y = jnp.concatenate([y_scr[lt] for lt in range(n_lt)], axis=1)
    u = _norm_mod(h, g_ref[...], _mod_row(mod_ref, 0), _mod_row(mod_ref, 1))
    z = _gelu_tanh(y + d_ref[...] * u)
    acc = jnp.dot(z.astype(BF16), w_ref[...], preferred_element_type=F32) + b_ref[...]
    o_ref[...] = h + _mod_row(mod_ref, 2) * (z * _sigmoid(acc))


def _glu_call(h, yg, g, mod, dvec, w_glu_b, b_glu, tpb, nb):
    n, d = h.shape
    groups = d // S5_GROUP
    return pl.pallas_call(
        _glu_kernel,
        grid=(n // TM,),
        in_specs=[_row_spec(d, tpb),
                  pl.BlockSpec((groups, TILE_CHUNKS, S5_CHUNK * S5_GROUP), lambda i: (0, i, 0)),
                  _const_spec((1, d)), _mod_spec(d, tpb, nb),
                  _const_spec((1, d)), _const_spec((d, d)), _const_spec((1, d))],
        out_specs=_row_spec(d, tpb),
        out_shape=jax.ShapeDtypeStruct((n, d), F32),
        scratch_shapes=[pltpu.VMEM((d // LANES, TM, LANES), F32)],
        compiler_params=_cparams(("parallel",)),
        name="s5_glu",
    )(h, yg, g.reshape(1, d), mod, dvec.reshape(1, d), w_glu_b, b_glu.reshape(1, d))


def _router_kernel(h_ref, g_ref, mod_ref, wr_ref, br_ref, ids_ref, gate_ref):
    v = _norm_mod(h_ref[...], g_ref[...], _mod_row(mod_ref, 3), _mod_row(mod_ref, 4))
    v_hi = v.astype(BF16)
    v_lo = (v - v_hi.astype(F32)).astype(BF16)
    both = jnp.dot(v_hi, wr_ref[...], preferred_element_type=F32)
    lg = (both[:, :LANES] + both[:, LANES:]
          + jnp.dot(v_lo, wr_ref[:, :LANES], preferred_element_type=F32) + br_ref[...])
    lane = lax.broadcasted_iota(jnp.int32, lg.shape, 1)
    neg = -jnp.inf
    big = jnp.int32(LANES)
    gl = jnp.where(lane < N_GROUPS, lg, neg)
    gmax = jnp.max(gl, axis=-1, keepdims=True)
    gidx = jnp.min(jnp.where(gl == gmax, lane, big), axis=-1, keepdims=True)
    g_w = 1.0 / jnp.sum(jnp.exp(gl - gmax), axis=-1, keepdims=True)
    lo = N_GROUPS + EXPERTS_PER_GROUP * gidx
    el = jnp.where((lane >= lo) & (lane < lo + EXPERTS_PER_GROUP), lg, neg)
    m1 = jnp.max(el, axis=-1, keepdims=True)
    i1 = jnp.min(jnp.where(el == m1, lane, big), axis=-1, keepdims=True)
    el2 = jnp.where(lane == i1, neg, el)
    m2 = jnp.max(el2, axis=-1, keepdims=True)
    i2 = jnp.min(jnp.where(el2 == m2, lane, big), axis=-1, keepdims=True)
    dd = jnp.exp(m2 - m1)
    w1 = g_w / (1.0 + dd)
    w2 = g_w * dd / (1.0 + dd)
    ids_ref[...] = jnp.where(lane == 0, i1 - N_GROUPS, jnp.where(lane == 1, i2 - N_GROUPS, 0))
    gate_ref[...] = jnp.where(lane == 0, w1, jnp.where(lane == 1, w2, 0.0))


def _router_call(h, g, mod, wr, br, tpb, nb):
    n, d = h.shape
    return pl.pallas_call(
        _router_kernel,
        grid=(n // TM,),
        in_specs=[_row_spec(d, tpb), _const_spec((1, d)), _mod_spec(d, tpb, nb),
                  _const_spec((d, 2 * LANES)), _const_spec((1, LANES))],
        out_specs=[_row_spec(LANES, tpb), _row_spec(LANES, tpb)],
        out_shape=[jax.ShapeDtypeStruct((n, LANES), jnp.int32), jax.ShapeDtypeStruct((n, LANES), F32)],
        compiler_params=_cparams(("parallel",)),
        name="moe_router",
    )(h, g.reshape(1, d), mod, wr, br)


def _row_copy(src, src_row, dst, dst_row, sem):
    return pltpu.make_async_copy(src.at[pl.ds(src_row, 1), :], dst.at[pl.ds(dst_row, 1), :], sem)


def _dispatch_kernel(slot_ref, h_ref, g_ref, mod_ref, xs_hbm, vbuf, sem):
    i = pl.program_id(0)
    last = pl.num_programs(0) - 1
    cur = i % 2
    vbuf[cur] = _norm_mod(h_ref[...], g_ref[...], _mod_row(mod_ref, 3), _mod_row(mod_ref, 4))

    def body(r, carry):
        for kk in range(TOP_K):
            _row_copy(vbuf.at[cur], r, xs_hbm, slot_ref[0, 0, kk * TM + r], sem.at[cur]).start()
        return carry
    lax.fori_loop(0, TM, body, 0, unroll=8)

    def wait_all(b):
        for _ in range(TOP_K):
            pltpu.make_async_copy(vbuf.at[b], xs_hbm.at[pl.ds(0, TM), :], sem.at[b]).wait()

    @pl.when(i > 0)
    def _():
        wait_all(1 - cur)

    @pl.when(i == last)
    def _():
        wait_all(cur)


def _dispatch_call(h, g, mod, slots_kt, tpb, nb):
    n, d = h.shape
    return pl.pallas_call(
        _dispatch_kernel,
        grid=(n // TM,),
        in_specs=[pl.BlockSpec((1, 1, TOP_K * TM), lambda i: (i, 0, 0), memory_space=pltpu.SMEM),
                  _row_spec(d, tpb), _const_spec((1, d)), _mod_spec(d, tpb, nb)],
        out_specs=pl.BlockSpec(memory_space=pl.ANY),
        out_shape=jax.ShapeDtypeStruct((TOP_K * n, d), F32),
        scratch_shapes=[pltpu.VMEM((2, TM, d), F32), pltpu.SemaphoreType.DMA((2,))],
        compiler_params=_cparams(("arbitrary",)),
        name="moe_dispatch",
    )(slots_kt, h, g.reshape(1, d), mod)


def _ffn_kernel(vb_ref, ve_ref, vlo_ref, vhi_ref, nv_ref, x_ref, w1_ref, w3_ref, w2_ref, y_ref, w1b, w3b, w2b):
    v = pl.program_id(0)
    live = v < nv_ref[0]
    prev = jnp.maximum(v - 1, 0)
    new_expert = (v == 0) | (ve_ref[v] != ve_ref[prev])
    new_block = (v == 0) | (vb_ref[v] != vb_ref[prev])

    @pl.when(new_expert & live)
    def _():
        w1b[...] = w1_ref[0, 0].astype(BF16)
        w3b[...] = w3_ref[0, 0].astype(BF16)
        w2b[...] = w2_ref[0, 0].astype(BF16)

    @pl.when(live)
    def _():
        x = x_ref[...].astype(BF16)
        h1 = jnp.dot(x, w1b[...], preferred_element_type=F32)
        h3 = jnp.dot(x, w3b[...], preferred_element_type=F32)
        act = (h1 * _sigmoid(h1)) * h3
        y = jnp.dot(act.astype(BF16), w2b[...], preferred_element_type=F32)
        row = lax.broadcasted_iota(jnp.int32, y.shape, 0)
        mine = (row >= vlo_ref[v]) & (row < vhi_ref[v])
        y_ref[...] = jnp.where(mine, y, jnp.where(new_block, 0.0, y_ref[...]))


def _ffn_call(xs, vis, w1, w3, w2, layer):
    n2, d = xs.shape
    de = w1.shape[-1]
    n_vis = vis[0].shape[0]
    grid_spec = pltpu.PrefetchScalarGridSpec(
        num_scalar_prefetch=5,
        grid=(n_vis,),
        in_specs=[
            pl.BlockSpec((MOE_TILE, d), lambda v, vb, ve, lo, hi, nv: (vb[v], 0)),
            pl.BlockSpec((1, 1, d, de), lambda v, vb, ve, lo, hi, nv: (layer, ve[v], 0, 0)),
            pl.BlockSpec((1, 1, d, de), lambda v, vb, ve, lo, hi, nv: (layer, ve[v], 0, 0)),
            pl.BlockSpec((1, 1, de, d), lambda v, vb, ve, lo, hi, nv: (layer, ve[v], 0, 0)),
        ],
        out_specs=pl.BlockSpec((MOE_TILE, d), lambda v, vb, ve, lo, hi, nv: (vb[v], 0)),
        scratch_shapes=[pltpu.VMEM((d, de), BF16), pltpu.VMEM((d, de), BF16), pltpu.VMEM((de, d), BF16)],
    )
    return pl.pallas_call(
        _ffn_kernel,
        grid_spec=grid_spec,
        out_shape=jax.ShapeDtypeStruct((n2, d), F32),
        compiler_params=_cparams(("arbitrary",)),
        name="moe_ffn",
    )(*vis, xs, w1, w3, w2)


def _combine_kernel(s0_ref, sn_ref, yb_hbm, h_ref, gate_ref, mod_ref, o_ref, ybuf, sem):
    i = pl.program_id(0)
    n_steps = pl.num_programs(0)
    cur = i % 2

    def gather(idx_ref, b):
        def body(r, carry):
            _row_copy(yb_hbm, idx_ref[0, 0, r], ybuf.at[b], r, sem.at[b]).start()
            return carry
        lax.fori_loop(0, TOP_K * TM, body, 0, unroll=8)

    @pl.when(i == 0)
    def _():
        gather(s0_ref, 0)

    @pl.when(i + 1 < n_steps)
    def _():
        gather(sn_ref, 1 - cur)

    pltpu.make_async_copy(yb_hbm.at[pl.ds(0, TOP_K * TM), :], ybuf.at[cur], sem.at[cur]).wait()
    gate = gate_ref[...]
    y = gate[:, 0:1] * ybuf[cur, 0:TM, :] + gate[:, 1:2] * ybuf[cur, TM:2 * TM, :]
    o_ref[...] = h_ref[...] + _mod_row(mod_ref, 5) * y


def _combine_call(h, yb, slots_kt, gates, mod, tpb, nb):
    n, d = h.shape
    n_tiles = n // TM
    return pl.pallas_call(
        _combine_kernel,
        grid=(n_tiles,),
        in_specs=[
            pl.BlockSpec((1, 1, TOP_K * TM), lambda i: (0, 0, 0), memory_space=pltpu.SMEM),
            pl.BlockSpec((1, 1, TOP_K * TM), lambda i: (jnp.minimum(i + 1, n_tiles - 1), 0, 0),
                         memory_space=pltpu.SMEM),
            pl.BlockSpec(memory_space=pl.ANY),
            _row_spec(d, tpb), _row_spec(LANES, tpb), _mod_spec(d, tpb, nb),
        ],
        out_specs=_row_spec(d, tpb),
        out_shape=jax.ShapeDtypeStruct((n, d), F32),
        scratch_shapes=[pltpu.VMEM((2, TOP_K * TM, d), F32), pltpu.SemaphoreType.DMA((2,))],
        compiler_params=_cparams(("arbitrary",)),
        name="moe_combine",
    )(slots_kt, slots_kt, yb, h, gates, mod)


def _routing_tables(ids, n_tok):
    n = n_tok * TOP_K
    expert = ids[:, :TOP_K].reshape(n)
    e_iota = jnp.arange(N_EXPERTS, dtype=jnp.int32)
    onehot = expert[:, None] == e_iota[None, :]
    csum = jnp.cumsum(onehot.astype(jnp.int32), axis=0)
    counts = csum[-1]
    send = jnp.cumsum(counts)
    sstart = send - counts
    rank = jnp.sum(jnp.where(onehot, csum, 0), axis=1) - 1
    row_of = jnp.sum(jnp.where(onehot, sstart[None, :], 0), axis=1) + rank
    n_tiles = n_tok // TM
    slots_kt = row_of.reshape(n_tiles, TM, TOP_K).transpose(0, 2, 1).reshape(n_tiles, 1, TOP_K * TM)

    first_blk = sstart // MOE_TILE
    last_blk = jnp.maximum(send - 1, 0) // MOE_TILE
    nvis = jnp.where(counts > 0, last_blk - first_blk + 1, 0)
    vend = jnp.cumsum(nvis)
    vstart = vend - nvis
    n_vis_max = n // MOE_TILE + N_EXPERTS
    vi = jnp.arange(n_vis_max, dtype=jnp.int32)
    total = vend[-1]
    vc = jnp.minimum(vi, total - 1)
    ve = jnp.sum(vc[:, None] >= vend[None, :], axis=1).astype(jnp.int32)
    sel = ve[:, None] == e_iota[None, :]
    pick = lambda t: jnp.sum(jnp.where(sel, t[None, :], 0), axis=1)
    vb = pick(first_blk) + vc - pick(vstart)
    lo = jnp.maximum(pick(sstart), vb * MOE_TILE) - vb * MOE_TILE
    hi = jnp.minimum(pick(send), (vb + 1) * MOE_TILE) - vb * MOE_TILE
    vis = tuple(t.astype(jnp.int32) for t in (vb, ve, lo, hi, total.reshape(1)))
    return slots_kt.astype(jnp.int32), vis


def _moe_layer(h, g2n, mod, w_grp, b_grp, w_rt, b_rt, w1, w3, w2, layer, tpb, nb):
    n, d = h.shape
    pad = LANES - N_GROUPS - N_EXPERTS
    wr = jnp.concatenate([w_grp, w_rt, jnp.zeros((d, pad), F32)], axis=1)
    wr_hi = wr.astype(BF16)
    wr = jnp.concatenate([wr_hi, (wr - wr_hi.astype(F32)).astype(BF16)], axis=1)
    br = jnp.concatenate([b_grp, b_rt, jnp.zeros((pad,), F32)]).reshape(1, LANES)
    ids, gates = _router_call(h, g2n, mod, wr, br, tpb, nb)
    slots_kt, vis = _routing_tables(ids, n)
    xs = _dispatch_call(h, g2n, mod, slots_kt, tpb, nb)
    yb = _ffn_call(xs, vis, w1, w3, w2, layer)
    return _combine_call(h, yb, slots_kt, gates, mod, tpb, nb)


def _s5_layer(h, g1n, mod, prm, tpb, nb):
    n, d = h.shape
    nt = n // nb
    n_chunks = nt // S5_CHUNK
    xg = _s5_in_call(h, g1n, mod, tpb, nb)
    mt, pm, qm, a = _s5_weights(prm['lam_re'], prm['lam_im'], prm['log_dt'], prm['b_re'], prm['b_im'],
                                prm['c_re'], prm['c_im'])
    n_lat_chunks = (nt - prm['ctx_len']) // S5_CHUNK
    yg = _s5_core_call(xg, mt, pm, qm, a, n_chunks, n_lat_chunks, nb)
    return _glu_call(h, yg, g1n, mod, prm['d'], prm['w_glu'].astype(BF16), prm['b_glu'], tpb, nb)


def _rope_tables(seq, ctx_len):
    rows = seq // GRID_W
    t_row = jnp.repeat(jnp.arange(rows), GRID_W).astype(F32)
    t_col = jnp.tile(jnp.arange(GRID_W), rows).astype(F32)
    half = HEAD_DIM // 2
    inv = ROPE_BASE ** (-jnp.arange(0, half, 2, dtype=F32) / half)
    ar = t_row[:, None] * inv
    ac = t_col[:, None] * inv
    ang = jnp.concatenate([ar, ar, ac, ac], axis=-1)
    cos = jnp.cos(ang)
    sin = jnp.sin(ang)
    lane = jnp.arange(HEAD_DIM)
    sin = jnp.where((lane & 32) == 0, -sin, sin)
    cos = jnp.concatenate([cos, jnp.ones((ctx_len, HEAD_DIM), F32)], axis=0)
    sin = jnp.concatenate([sin, jnp.zeros((ctx_len, HEAD_DIM), F32)], axis=0)
    return cos, sin


def kernel(x, c, ctx, c_ctx, w_mod, b_mod, norm1_g, norm2_g, s5_lam_re, s5_lam_im, s5_log_dt, s5_b_re, s5_b_im, s5_c_re, s5_c_im, s5_d, s5_w_glu, s5_b_glu, swa_w_qkv, swa_q_g, swa_k_g, swa_sink, swa_w_o, ga_w_qkv, ga_q_g, ga_k_g, ga_w_o, moe_w_grp, moe_b_grp, moe_w_rt, moe_b_rt, moe_w1, moe_w3, moe_w2):
    nb, seq, d = x.shape
    ctx_len = ctx.shape[1]
    depth = w_mod.shape[0]
    nt = seq + ctx_len
    n = nb * nt
    tpb = nt // TM
    assert nt % TM == 0 and ctx_len == TM and nb < 8 and (TOP_K * n) % MOE_TILE == 0 and d % (S5_SUB * S5_GROUP) == 0

    c8 = jnp.zeros((8, d), F32).at[:nb].set(c).at[nb].set(c_ctx)
    mods = _ada_mod_all(c8, w_mod, b_mod).reshape(depth, 8, N_MOD, d)
    cos, sin = _rope_tables(seq, ctx_len)

    h = jnp.concatenate([x, ctx], axis=1).reshape(n, d)
    for i in range(depth):
        kind, j = i % 3, i // 3
        mod = mods[i]
        if kind == 0:
            prm = dict(lam_re=s5_lam_re[j], lam_im=s5_lam_im[j], log_dt=s5_log_dt[j], b_re=s5_b_re[j],
                       b_im=s5_b_im[j], c_re=s5_c_re[j], c_im=s5_c_im[j], d=s5_d[j], w_glu=s5_w_glu[j],
                       b_glu=s5_b_glu[j], ctx_len=ctx_len)
            h = _s5_layer(h, norm1_g[i], mod, prm, tpb, nb)
        elif kind == 1:
            q, k, v = _qkv_call(h, norm1_g[i], mod, cos, sin, swa_q_g[j], swa_k_g[j],
                                swa_w_qkv[j].astype(BF16), tpb, nb)
            o = _swa_call(q, k, v, swa_sink[j], nb, seq, ctx_len)
            h = _oproj_call(o, swa_w_o[j].astype(BF16), h, mod, tpb, nb)
        else:
            q, k, v = _qkv_call(h, norm1_g[i], mod, cos, sin, ga_q_g[j], ga_k_g[j],
                                ga_w_qkv[j].astype(BF16), tpb, nb)
            o = _ga_call(q, k, v, nb, seq, ctx_len)
            h = _oproj_call(o, ga_w_o[j].astype(BF16), h, mod, tpb, nb)
        h = _moe_layer(h, norm2_g[i], mod, moe_w_grp[i], moe_b_grp[i], moe_w_rt[i], moe_b_rt[i],
                       moe_w1, moe_w3, moe_w2, i, tpb, nb)
    return h.reshape(nb, nt, d)[:, :seq]
```

```python
import functools
import math

import jax
import jax.numpy as jnp
from jax import lax
from jax.experimental import pallas as pl
from jax.experimental.pallas import tpu as pltpu

F32 = jnp.float32
BF16 = jnp.bfloat16

EPS = 1e-6
N_MOD = 6
HEAD_DIM = 128
N_HEADS = 16
N_KV_HEADS = 4
REP = N_HEADS // N_KV_HEADS
WINDOW = 128
GRID_W = 64
ROPE_BASE = 10000.0
S5_GROUP = 16
S5_STATE = 64
N_GROUPS = 4
EXPERTS_PER_GROUP = 8
N_EXPERTS = N_GROUPS * EXPERTS_PER_GROUP
TOP_K = 2
ATTN_SCALE = HEAD_DIM ** -0.5

LANES = 128
TM = 256
S5_CHUNK = 16
MOE_TILE = 256
VMEM_LIMIT = 56 * 1024 * 1024


def _cparams(sem, vmem=VMEM_LIMIT):
    return pltpu.CompilerParams(dimension_semantics=sem, vmem_limit_bytes=vmem)


def _sigmoid(x):
    return 1.0 / (1.0 + jnp.exp(-x))


def _norm_mod(x, g, shift, scale):
    ms = jnp.mean(x * x, axis=-1, keepdims=True)
    y = x * lax.rsqrt(ms + EPS) * g
    return y * (1.0 + scale) + shift


def _mod_row(mod_ref, k):
    return mod_ref[0, k:k + 1, :]


def _mod_kernel(c_ref, w_ref, b_ref, o_ref):
    c = c_ref[...]
    cond = c * _sigmoid(c)
    acc = jnp.dot(cond.astype(BF16), w_ref[...].astype(BF16), preferred_element_type=F32)
    o_ref[...] = acc + b_ref[...]


def _ada_mod_all(c8, w_mod, b_mod):
    depth, d, n6 = w_mod.shape
    tn = 1536
    return pl.pallas_call(
        _mod_kernel,
        grid=(depth, n6 // tn),
        in_specs=[
            pl.BlockSpec((8, d), lambda l, j: (0, 0)),
            pl.BlockSpec((None, d, tn), lambda l, j: (l, 0, j)),
            pl.BlockSpec((None, 1, tn), lambda l, j: (l, 0, j)),
        ],
        out_specs=pl.BlockSpec((None, 8, tn), lambda l, j: (l, 0, j)),
        out_shape=jax.ShapeDtypeStruct((depth, 8, n6), F32),
        compiler_params=_cparams(("parallel", "parallel")),
        name="ada_mod",
    )(c8, w_mod, b_mod.reshape(depth, 1, n6))


def _row_spec(width, tpb):
    del tpb
    return pl.BlockSpec((TM, width), lambda i: (i, 0))


def _mod_spec(d, tpb, nb):
    return pl.BlockSpec((1, N_MOD, d), lambda i: (jnp.where(i % tpb == tpb - 1, nb, i // tpb), 0, 0))


def _const_spec(shape):
    nd = len(shape)
    return pl.BlockSpec(shape, lambda i: (0,) * nd)


S5_SUB = 8
TILE_CHUNKS = TM // S5_CHUNK


def _granule(shape):
    return lax.broadcasted_iota(jnp.int32, shape, 1) // S5_GROUP


def _s5_in_kernel(h_ref, g_ref, mod_ref, x_ref, u_scr):
    u = _norm_mod(h_ref[...], g_ref[...], _mod_row(mod_ref, 0), _mod_row(mod_ref, 1))
    n_lt = u.shape[1] // LANES
    for lt in range(n_lt):
        u_scr[lt] = u[:, lt * LANES:(lt + 1) * LANES]
    gran = _granule((TILE_CHUNKS, LANES))
    for lt in range(n_lt):
        for eta in range(S5_CHUNK // S5_SUB):
            rolled = []
            for dl in range(S5_SUB):
                r = u_scr[lt, pl.ds(S5_SUB * eta + dl, TILE_CHUNKS, stride=S5_CHUNK), :]
                rolled.append(pltpu.roll(r, dl * S5_GROUP, 1) if dl else r)
            for gl in range(S5_SUB):
                o = rolled[0]
                for dl in range(1, S5_SUB):
                    o = jnp.where(gran == (gl + dl) % S5_SUB, rolled[dl], o)
                x_ref[lt * S5_SUB + gl, :, eta * LANES:(eta + 1) * LANES] = o.astype(x_ref.dtype)


def _s5_in_call(h, g, mod, tpb, nb):
    n, d = h.shape
    groups = d // S5_GROUP
    return pl.pallas_call(
        _s5_in_kernel,
        grid=(n // TM,),
        in_specs=[_row_spec(d, tpb), _const_spec((1, d)), _mod_spec(d, tpb, nb)],
        out_specs=pl.BlockSpec((groups, TILE_CHUNKS, S5_CHUNK * S5_GROUP), lambda i: (0, i, 0)),
        out_shape=jax.ShapeDtypeStruct((groups, n // S5_CHUNK, S5_CHUNK * S5_GROUP), BF16),
        scratch_shapes=[pltpu.VMEM((d // LANES, TM, LANES), F32)],
        compiler_params=_cparams(("parallel",)),
        name="s5_in",
    )(h, g.reshape(1, d), mod)


def _qkv_kernel(h_ref, g_ref, mod_ref, cos_ref, sin_ref, qg_ref, kg_ref, w_ref, q_ref, k_ref, v_ref):
    u = _norm_mod(h_ref[...], g_ref[...], _mod_row(mod_ref, 0), _mod_row(mod_ref, 1))
    p = jnp.dot(u.astype(BF16), w_ref[...], preferred_element_type=F32)
    cos = cos_ref[...]
    sin = sin_ref[...]
    lane = lax.broadcasted_iota(jnp.int32, (TM, HEAD_DIM), 1)
    first = (lane & 32) == 0

    def head(xh, gain, scale):
        ms = jnp.mean(xh * xh, axis=-1, keepdims=True)
        xn = xh * lax.rsqrt(ms + EPS) * gain
        rot = jnp.where(first, pltpu.roll(xn, HEAD_DIM - 32, 1), pltpu.roll(xn, 32, 1))
        return (xn * cos + rot * sin) * scale

    qg = qg_ref[...]
    kg = kg_ref[...]
    nq = N_HEADS * HEAD_DIM
    nk = N_KV_HEADS * HEAD_DIM
    for hh in range(N_HEADS):
        sl = slice(hh * HEAD_DIM, (hh + 1) * HEAD_DIM)
        q_ref[:, sl] = head(p[:, sl], qg, ATTN_SCALE).astype(BF16)
    for hh in range(N_KV_HEADS):
        sl = slice(hh * HEAD_DIM, (hh + 1) * HEAD_DIM)
        k_ref[:, sl] = head(p[:, nq + hh * HEAD_DIM:nq + (hh + 1) * HEAD_DIM], kg, 1.0).astype(BF16)
    v_ref[...] = p[:, nq + nk:].astype(BF16)


def _qkv_call(h, g, mod, cos, sin, q_g, k_g, w_qkv_b, tpb, nb):
    n, d = h.shape
    nq = N_HEADS * HEAD_DIM
    nk = N_KV_HEADS * HEAD_DIM
    return pl.pallas_call(
        _qkv_kernel,
        grid=(n // TM,),
        in_specs=[
            _row_spec(d, tpb), _const_spec((1, d)), _mod_spec(d, tpb, nb),
            pl.BlockSpec((TM, HEAD_DIM), lambda i: (i % tpb, 0)),
            pl.BlockSpec((TM, HEAD_DIM), lambda i: (i % tpb, 0)),
            _const_spec((1, HEAD_DIM)), _const_spec((1, HEAD_DIM)),
            _const_spec((d, nq + 2 * nk)),
        ],
        out_specs=[_row_spec(nq, tpb), _row_spec(nk, tpb), _row_spec(nk, tpb)],
        out_shape=[jax.ShapeDtypeStruct((n, nq), BF16), jax.ShapeDtypeStruct((n, nk), BF16),
                   jax.ShapeDtypeStruct((n, nk), BF16)],
        compiler_params=_cparams(("parallel",)),
        name="qkv_proj",
    )(h, g.reshape(1, d), mod, cos, sin, q_g.reshape(1, HEAD_DIM), k_g.reshape(1, HEAD_DIM), w_qkv_b)


def _oproj_kernel(a_ref, w_ref, h_ref, mod_ref, o_ref):
    acc = jnp.dot(a_ref[...], w_ref[...], preferred_element_type=F32)
    o_ref[...] = h_ref[...] + _mod_row(mod_ref, 2) * acc


def _oproj_call(a, w_o_b, h, mod, tpb, nb):
    n, d = h.shape
    return pl.pallas_call(
        _oproj_kernel,
        grid=(n // TM,),
        in_specs=[_row_spec(a.shape[1], tpb), _const_spec(w_o_b.shape), _row_spec(d, tpb), _mod_spec(d, tpb, nb)],
        out_specs=_row_spec(d, tpb),
        out_shape=jax.ShapeDtypeStruct((n, d), F32),
        compiler_params=_cparams(("parallel",)),
        name="o_proj",
    )(a, w_o_b, h, mod)


GA_TQ = 256
GA_TK = 512
SWA_TQ = 256


def _stack_heads(q_ref):
    return jnp.concatenate([q_ref[:, r * HEAD_DIM:(r + 1) * HEAD_DIM] for r in range(REP)], axis=0)


def _unstack_heads(o, o_ref):
    tq = o_ref.shape[0]
    for r in range(REP):
        o_ref[:, r * HEAD_DIM:(r + 1) * HEAD_DIM] = o[r * tq:(r + 1) * tq, :].astype(o_ref.dtype)


def _with_ones(v):
    return jnp.concatenate([v, jnp.ones_like(v)], axis=1)


def _flash_step(q, k, v, m, acc):
    s = lax.dot_general(q, k, (((1,), (1,)), ((), ())), preferred_element_type=F32)
    m_new = jnp.maximum(m, jnp.max(s, axis=-1, keepdims=True))
    alpha = jnp.exp(m - m_new)
    p = jnp.exp(s - m_new[:, :1]).astype(BF16)
    pv = jnp.dot(p, _with_ones(v), preferred_element_type=F32)
    return m_new, jnp.concatenate([alpha, alpha], axis=1) * acc + pv


def _ga_kernel(q_ref, k_ref, v_ref, o_ref, *, n_lat_tiles, seq):
    qt = pl.program_id(2)
    q = _stack_heads(q_ref)
    rows = q.shape[0]
    m0 = jnp.full((rows, HEAD_DIM), -jnp.inf, F32)
    a0 = jnp.zeros((rows, 2 * HEAD_DIM), F32)

    def finish(acc):
        _unstack_heads(acc[:, :HEAD_DIM] / acc[:, HEAD_DIM:], o_ref)

    @pl.when(qt < n_lat_tiles)
    def _():
        m, acc = m0, a0
        for c in range(seq // GA_TK):
            m, acc = _flash_step(q, k_ref[c * GA_TK:(c + 1) * GA_TK, :], v_ref[c * GA_TK:(c + 1) * GA_TK, :], m, acc)
        m, acc = _flash_step(q, k_ref[seq:, :], v_ref[seq:, :], m, acc)
        finish(acc)

    @pl.when(qt >= n_lat_tiles)
    def _():
        finish(_flash_step(q, k_ref[seq:, :], v_ref[seq:, :], m0, a0)[1])


def _attn_specs(tq, nt):
    n_qt = nt // tq
    q_spec = pl.BlockSpec((tq, REP * HEAD_DIM), lambda b, g, i: (b * n_qt + i, g))
    kv_spec = pl.BlockSpec((nt, HEAD_DIM), lambda b, g, i: (b, g))
    return n_qt, q_spec, kv_spec


def _ga_call(q, k, v, nb, seq, ctx_len):
    n = q.shape[0]
    nt = seq + ctx_len
    assert seq % GA_TK == 0 and seq % GA_TQ == 0 and ctx_len % GA_TQ == 0
    n_qt, q_spec, kv_spec = _attn_specs(GA_TQ, nt)
    return pl.pallas_call(
        functools.partial(_ga_kernel, n_lat_tiles=seq // GA_TQ, seq=seq),
        grid=(nb, N_KV_HEADS, n_qt),
        in_specs=[q_spec, kv_spec, kv_spec],
        out_specs=q_spec,
        out_shape=jax.ShapeDtypeStruct((n, N_HEADS * HEAD_DIM), BF16),
        compiler_params=_cparams(("parallel", "parallel", "parallel")),
        name="ga_attn",
    )(q, k, v)


def _swa_kernel(sink_ref, q_ref, k_ref, v_ref, o_ref, *, n_lat_tiles, seq):
    g = pl.program_id(1)
    i = pl.program_id(2)
    tq = q_ref.shape[0]
    nwin = tq + 2 * WINDOW
    q = _stack_heads(q_ref)
    rows = q.shape[0]
    start = pl.multiple_of(jnp.clip(i * tq - WINDOW, 0, seq - nwin), WINDOW)
    k_all = jnp.concatenate([k_ref[pl.ds(start, nwin), :], k_ref[seq:, :]], axis=0)
    v_all = jnp.concatenate([v_ref[pl.ds(start, nwin), :], v_ref[seq:, :]], axis=0)
    nk = k_all.shape[0]
    col = lax.broadcasted_iota(jnp.int32, (rows, nk), 1)
    row = lax.broadcasted_iota(jnp.int32, (rows, nk), 0)
    qbase = jnp.where(i < n_lat_tiles, i * tq, -(1 << 20))
    qpos = qbase + (row & (tq - 1))
    mask = (jnp.abs(qpos - (start + col)) <= WINDOW) | (col >= nwin)
    s = lax.dot_general(q, k_all, (((1,), (1,)), ((), ())), preferred_element_type=F32)
    s = jnp.where(mask, s, -jnp.inf)
    head = lax.broadcasted_iota(jnp.int32, (rows, HEAD_DIM), 0) // tq
    sk = jnp.zeros((rows, HEAD_DIM), F32)
    for r in range(REP):
        sk = jnp.where(head == r, sink_ref[g * REP + r], sk)
    m = jnp.maximum(jnp.max(s, axis=-1, keepdims=True), sk)
    p = jnp.exp(s - m[:, :1]).astype(BF16)
    acc = jnp.dot(p, _with_ones(v_all), preferred_element_type=F32)
    _unstack_heads(acc[:, :HEAD_DIM] / (acc[:, HEAD_DIM:] + jnp.exp(sk - m)), o_ref)


def _swa_call(q, k, v, sink, nb, seq, ctx_len):
    n = q.shape[0]
    nt = seq + ctx_len
    assert seq % SWA_TQ == 0 and ctx_len % SWA_TQ == 0 and SWA_TQ % WINDOW == 0 and SWA_TQ & (SWA_TQ - 1) == 0
    n_qt, q_spec, kv_spec = _attn_specs(SWA_TQ, nt)
    return pl.pallas_call(
        functools.partial(_swa_kernel, n_lat_tiles=seq // SWA_TQ, seq=seq),
        grid=(nb, N_KV_HEADS, n_qt),
        in_specs=[pl.BlockSpec(memory_space=pltpu.SMEM), q_spec, kv_spec, kv_spec],
        out_specs=q_spec,
        out_shape=jax.ShapeDtypeStruct((n, N_HEADS * HEAD_DIM), BF16),
        compiler_params=_cparams(("parallel", "parallel", "parallel")),
        name="swa_attn",
    )(sink, q, k, v)


def _s5_kernel(x_ref, mt_ref, p_ref, q_ref, a_ref, y_ref, s_scr, h_scr, *, n_chunks, n_lat_chunks, nb):
    half = S5_STATE
    for b in range(nb):
        sb = jnp.dot(x_ref[0, b * n_chunks:(b + 1) * n_chunks, :], p_ref[0], preferred_element_type=F32)
        s_scr[0, pl.ds(b, n_chunks, stride=nb), :] = sb[:, :2 * half]
        s_scr[1, pl.ds(b, n_chunks, stride=nb), :] = sb[:, 2 * half:]
    ar = jnp.broadcast_to(a_ref[0, 0:1, :], (nb, 2 * half))
    ai = jnp.broadcast_to(a_ref[0, 1:2, :], (nb, 2 * half))
    is_fwd = lax.broadcasted_iota(jnp.int32, (nb, 2 * half), 1) < half
    hr = jnp.zeros((nb, 2 * half), F32)
    hi = jnp.zeros((nb, 2 * half), F32)
    for step in range(n_chunks):
        rf = nb * ((step + n_lat_chunks) % n_chunks)
        rb = nb * (n_chunks - 1 - step)
        h_scr[0, rf:rf + nb, 0:half] = hr[:, 0:half]
        h_scr[1, rf:rf + nb, 0:half] = hi[:, 0:half]
        h_scr[0, rb:rb + nb, half:] = hr[:, half:]
        h_scr[1, rb:rb + nb, half:] = hi[:, half:]
        sr = jnp.where(is_fwd, s_scr[0, rf:rf + nb, :], s_scr[0, rb:rb + nb, :])
        si = jnp.where(is_fwd, s_scr[1, rf:rf + nb, :], s_scr[1, rb:rb + nb, :])
        hr, hi = ar * hr - ai * hi + sr, ar * hi + ai * hr + si
    for b in range(nb):
        hin = jnp.concatenate([h_scr[0, pl.ds(b, n_chunks, stride=nb), :],
                               h_scr[1, pl.ds(b, n_chunks, stride=nb), :]], axis=1).astype(BF16)
        y = jnp.dot(x_ref[0, b * n_chunks:(b + 1) * n_chunks, :], mt_ref[0], preferred_element_type=F32)
        y_ref[0, b * n_chunks:(b + 1) * n_chunks, :] = y + jnp.dot(hin, q_ref[0], preferred_element_type=F32)


def _s5_core_call(xg, mt, pm, qm, a, n_chunks, n_lat_chunks, nb):
    g, rows, w = xg.shape
    return pl.pallas_call(
        functools.partial(_s5_kernel, n_chunks=n_chunks, n_lat_chunks=n_lat_chunks, nb=nb),
        grid=(g,),
        in_specs=[
            pl.BlockSpec((1, rows, w), lambda i: (i, 0, 0)),
            pl.BlockSpec((1, w, w), lambda i: (i, 0, 0)),
            pl.BlockSpec((1, w, 4 * S5_STATE), lambda i: (i, 0, 0)),
            pl.BlockSpec((1, 4 * S5_STATE, w), lambda i: (i, 0, 0)),
            pl.BlockSpec((1, 2, 2 * S5_STATE), lambda i: (i, 0, 0)),
        ],
        out_specs=pl.BlockSpec((1, rows, w), lambda i: (i, 0, 0)),
        out_shape=jax.ShapeDtypeStruct((g, rows, w), F32),
        scratch_shapes=[pltpu.VMEM((2, rows, 2 * S5_STATE), F32), pltpu.VMEM((2, rows, 2 * S5_STATE), F32)],
        compiler_params=_cparams(("parallel",)),
        name="s5_core",
    )(xg, mt, pm, qm, a)


def _cplx_outer(pw_ref, x_ref, d):
    o = (pw_ref[0, d, 0][:, None, :] * x_ref[0, d, 0][None, :, :]
         + pw_ref[0, d, 1][:, None, :] * x_ref[0, d, 1][None, :, :])
    return o.reshape(S5_CHUNK * S5_GROUP, 2 * S5_STATE)


def _state_lanes(f, b, im_sign):
    first = lax.broadcasted_iota(jnp.int32, f.shape, 1) < S5_STATE
    re = jnp.where(first, f, pltpu.roll(b, S5_STATE, 1))
    im = jnp.where(first, pltpu.roll(f, S5_STATE, 1), b)
    return jnp.concatenate([re, im_sign * im], axis=1)


def _s5_ops_kernel(pu_ref, pv_ref, pp_ref, pq_ref, b_ref, c_ref, mask_ref, mt_ref, pm_ref, qm_ref):
    conj = jnp.where(lax.broadcasted_iota(jnp.int32, (1, 2 * S5_STATE), 1) < S5_STATE, 1.0, -1.0)
    mt = jnp.zeros(mt_ref.shape[1:], F32)
    for d in range(2):
        u = _cplx_outer(pu_ref, b_ref, d)
        w = _cplx_outer(pv_ref, c_ref, d) * conj
        mt = mt + mask_ref[0, d] * lax.dot_general(u, w, (((1,), (1,)), ((), ())), precision=lax.Precision.HIGHEST,
                                                    preferred_element_type=F32)
    mt_ref[0] = mt.astype(mt_ref.dtype)
    pm_ref[0] = _state_lanes(_cplx_outer(pp_ref, b_ref, 0), _cplx_outer(pp_ref, b_ref, 1), 1.0).astype(pm_ref.dtype)
    qt = _state_lanes(_cplx_outer(pq_ref, c_ref, 0), _cplx_outer(pq_ref, c_ref, 1), -1.0)
    qm_ref[0] = qt.T.astype(qm_ref.dtype)


def _s5_weights(lam_re, lam_im, log_dt, b_re, b_im, c_re, c_im):
    t = S5_CHUNK
    lam = lax.complex(lam_re.astype(F32), lam_im.astype(F32))
    dt = jnp.exp(log_dt.astype(F32))[..., None]
    ldt = lam * dt
    lam_bar = jnp.exp(ldt)
    b_bar = ((lam_bar - 1) / lam)[..., None] * lax.complex(b_re.astype(F32), b_im.astype(F32))
    cm = lax.complex(c_re.astype(F32), c_im.astype(F32))
    g = lam_re.shape[1]
    pos = jnp.arange(t)
    tpos = S5_SUB * (pos[None, :] // S5_SUB) + (pos[None, :] % S5_SUB - jnp.arange(S5_SUB)[:, None]) % S5_SUB
    tg = jnp.tile(tpos, (g // S5_SUB, 1)).astype(F32)
    mid = t // 2
    sign = jnp.array([1.0, -1.0], F32)[:, None, None]

    def powers(expo):
        z = jnp.exp(ldt[:, :, None, :] * expo[..., None])
        planes = jnp.stack([jnp.concatenate([z.real, z.real], -1), jnp.concatenate([-z.imag, z.imag], -1)], axis=2)
        return planes.transpose(1, 0, 2, 3, 4)

    def operand(z):
        planes = jnp.stack([jnp.concatenate([z.real, z.imag], -1), jnp.concatenate([z.imag, z.real], -1)], axis=2)
        return planes.transpose(1, 0, 2, 3, 4)

    pu = powers(sign * (mid - tg)[None])
    pv = powers(sign * (tg - mid)[None])
    pp = powers(jnp.stack([t - 1 - tg, tg]))
    pq = powers(jnp.stack([tg + 1, t - tg]))
    bo = operand(b_bar.transpose(0, 1, 3, 2))
    co = operand(cm)
    lane_t = jnp.repeat(tpos, S5_GROUP, axis=1)
    causal = lane_t[:, None, :] >= lane_t[:, :, None]
    mask = jnp.stack([causal, lane_t[:, None, :] <= lane_t[:, :, None]], axis=1).astype(F32)
    w = t * S5_GROUP
    blk = lambda *shape: pl.BlockSpec((1,) + shape, lambda i: (i,) + (0,) * len(shape))
    mt, pm, qm = pl.pallas_call(
        _s5_ops_kernel,
        grid=(g,),
        in_specs=[blk(2, 2, t, LANES)] * 4 + [blk(2, 2, S5_GROUP, LANES)] * 2
                 + [pl.BlockSpec((1, 2, w, w), lambda i: (i % S5_SUB, 0, 0, 0))],
        out_specs=[blk(w, w), blk(w, 4 * S5_STATE), blk(4 * S5_STATE, w)],
        out_shape=[jax.ShapeDtypeStruct((g, w, w), BF16), jax.ShapeDtypeStruct((g, w, 4 * S5_STATE), BF16),
                   jax.ShapeDtypeStruct((g, 4 * S5_STATE, w), BF16)],
        compiler_params=_cparams(("parallel",)),
        name="s5_ops",
    )(pu, pv, pp, pq, bo, co, mask)
    at = jnp.exp(ldt * t)
    a = jnp.stack([jnp.concatenate([at[0].real, at[1].real], axis=-1),
                   jnp.concatenate([at[0].imag, at[1].imag], axis=-1)], axis=1)
    return mt, pm, qm, a.astype(F32)


def _gelu_tanh(x):
    return 0.5 * x * (1.0 + jnp.tanh(math.sqrt(2.0 / math.pi) * (x + 0.044715 * (x * x * x))))


def _glu_kernel(h_ref, y_ref, g_ref, mod_ref, d_ref, w_ref, b_ref, o_ref, y_scr):
    h = h_ref[...]
    n_lt = h.shape[1] // LANES
    gran = _granule((TILE_CHUNKS, LANES))
    for lt in range(n_lt):
        for eta in range(S5_CHUNK // S5_SUB):
            ys = [y_ref[lt * S5_SUB + gl, :, eta * LANES:(eta + 1) * LANES] for gl in range(S5_SUB)]
            for dl in range(S5_SUB):
                o = ys[0]
                for gl in range(1, S5_SUB):
                    o = jnp.where(gran == (gl + dl) % S5_SUB, ys[gl], o)
                if dl:
                    o = pltpu.roll(o, LANES - dl * S5_GROUP, 1)
                y_scr[lt, pl.ds(S5_SUB * eta + dl, TILE_CHUNKS, stride=S5_CHUNK), :] = o
    y = jnp.concatenate([y_scr[lt] for lt in range(n_lt)], axis=1)
    u = _norm_mod(h, g_ref[...], _mod_row(mod_ref, 0), _mod_row(mod_ref, 1))
    z = _gelu_tanh(y + d_ref[...] * u)
    acc = jnp.dot(z.astype(BF16), w_ref[...], preferred_element_type=F32) + b_ref[...]
    o_ref[...] = h + _mod_row(mod_ref, 2) * (z * _sigmoid(acc))


def _glu_call(h, yg, g, mod, dvec, w_glu_b, b_glu, tpb, nb):
    n, d = h.shape
    groups = d // S5_GROUP
    return pl.pallas_call(
        _glu_kernel,
        grid=(n // TM,),
        in_specs=[_row_spec(d, tpb),
                  pl.BlockSpec((groups, TILE_CHUNKS, S5_CHUNK * S5_GROUP), lambda i: (0, i, 0)),
                  _const_spec((1, d)), _mod_spec(d, tpb, nb),
                  _const_spec((1, d)), _const_spec((d, d)), _const_spec((1, d))],
        out_specs=_row_spec(d, tpb),
        out_shape=jax.ShapeDtypeStruct((n, d), F32),
        scratch_shapes=[pltpu.VMEM((d // LANES, TM, LANES), F32)],
        compiler_params=_cparams(("parallel",)),
        name="s5_glu",
    )(h, yg, g.reshape(1, d), mod, dvec.reshape(1, d), w_glu_b, b_glu.reshape(1, d))


def _pack_bf16_pairs(x):
    w = x.shape[1] // 2
    lo = pltpu.bitcast(x[:, :w].astype(BF16).astype(F32), jnp.uint32)
    hi = pltpu.bitcast(x[:, w:].astype(BF16).astype(F32), jnp.uint32)
    return (hi & jnp.uint32(0xFFFF0000)) | (lo >> 16)


def _unpack_bf16_pairs(p):
    lo = pltpu.bitcast(p << 16, F32)
    hi = pltpu.bitcast(p & jnp.uint32(0xFFFF0000), F32)
    return jnp.concatenate([lo, hi], axis=1).astype(BF16)


def _router_kernel(h_ref, g_ref, mod_ref, wr_ref, br_ref, ids_ref, gate_ref, vp_ref):
    v = _norm_mod(h_ref[...], g_ref[...], _mod_row(mod_ref, 3), _mod_row(mod_ref, 4))
    v_hi = v.astype(BF16)
    v_lo = (v - v_hi.astype(F32)).astype(BF16)
    vp_ref[...] = _pack_bf16_pairs(v)
    both = jnp.dot(v_hi, wr_ref[...], preferred_element_type=F32)
    lg = (both[:, :LANES] + both[:, LANES:]
          + jnp.dot(v_lo, wr_ref[:, :LANES], preferred_element_type=F32) + br_ref[...])
    lane = lax.broadcasted_iota(jnp.int32, lg.shape, 1)
    neg = -jnp.inf
    big = jnp.int32(LANES)
    gl = jnp.where(lane < N_GROUPS, lg, neg)
    gmax = jnp.max(gl, axis=-1, keepdims=True)
    gidx = jnp.min(jnp.where(gl == gmax, lane, big), axis=-1, keepdims=True)
    g_w = 1.0 / jnp.sum(jnp.exp(gl - gmax), axis=-1, keepdims=True)
    lo = N_GROUPS + EXPERTS_PER_GROUP * gidx
    el = jnp.where((lane >= lo) & (lane < lo + EXPERTS_PER_GROUP), lg, neg)
    m1 = jnp.max(el, axis=-1, keepdims=True)
    i1 = jnp.min(jnp.where(el == m1, lane, big), axis=-1, keepdims=True)
    el2 = jnp.where(lane == i1, neg, el)
    m2 = jnp.max(el2, axis=-1, keepdims=True)
    i2 = jnp.min(jnp.where(el2 == m2, lane, big), axis=-1, keepdims=True)
    dd = jnp.exp(m2 - m1)
    w1 = g_w / (1.0 + dd)
    w2 = g_w * dd / (1.0 + dd)
    ids_ref[...] = jnp.where(lane == 0, i1 - N_GROUPS, jnp.where(lane == 1, i2 - N_GROUPS, 0))
    gate_ref[...] = jnp.where(lane == 0, w1, jnp.where(lane == 1, w2, 0.0))


def _router_call(h, g, mod, wr, br, tpb, nb):
    n, d = h.shape
    return pl.pallas_call(
        _router_kernel,
        grid=(n // TM,),
        in_specs=[_row_spec(d, tpb), _const_spec((1, d)), _mod_spec(d, tpb, nb),
                  _const_spec((d, 2 * LANES)), _const_spec((1, LANES))],
        out_specs=[_row_spec(LANES, tpb), _row_spec(LANES, tpb), _row_spec(d // 2, tpb)],
        out_shape=[jax.ShapeDtypeStruct((n, LANES), jnp.int32), jax.ShapeDtypeStruct((n, LANES), F32),
                   jax.ShapeDtypeStruct((n, d // 2), jnp.uint32)],
        compiler_params=_cparams(("parallel",)),
        name="moe_router",
    )(h, g.reshape(1, d), mod, wr, br)


def _ffn_kernel(vb_ref, ve_ref, vlo_ref, vhi_ref, nv_ref,
                tokc_ref, tokn_ref, dstp_ref, dstc_ref, v_hbm, w1_ref, w3_ref, w2_ref, yk_hbm,
                x0, x1, y0, y1, w13b, w2b, sem_in, sem_out, *, n_blocks):
    v = pl.program_id(0)
    n_live = nv_ref[0]
    live = v < n_live
    blk = vb_ref[v]
    prev = jnp.maximum(v - 1, 0)
    new_expert = (v == 0) | (ve_ref[v] != ve_ref[prev])
    new_block = (v == 0) | (blk != vb_ref[prev])
    odd = (blk % 2) == 1
    de = w1_ref.shape[-1]

    def gather(idx_ref, x, sem):
        for r in range(MOE_TILE):
            pltpu.make_async_copy(v_hbm.at[pl.ds(idx_ref[0, 0, r], 1), :], x.at[pl.ds(r, 1), :], sem).start()

    def scatter(y, idx_ref, sem):
        for r in range(MOE_TILE):
            pltpu.make_async_copy(y.at[pl.ds(r, 1), :], yk_hbm.at[pl.ds(idx_ref[0, 0, r], 1), :], sem).start()

    def wait_in(x, sem):
        pltpu.make_async_copy(v_hbm.at[pl.ds(0, MOE_TILE), :], x, sem).wait()

    def wait_out(y, sem):
        pltpu.make_async_copy(y, yk_hbm.at[pl.ds(0, MOE_TILE), :], sem).wait()

    @pl.when(new_expert & live)
    def _():
        w13b[:, :de] = w1_ref[0, 0].astype(BF16)
        w13b[:, de:] = w3_ref[0, 0].astype(BF16)
        w2b[...] = w2_ref[0, 0].astype(BF16)

    def ffn(x, y, first):
        h13 = jnp.dot(_unpack_bf16_pairs(x[...]), w13b[...], preferred_element_type=F32)
        h1 = h13[:, :de]
        act = (h1 * _sigmoid(h1)) * h13[:, de:]
        out = jnp.dot(act.astype(BF16), w2b[...], preferred_element_type=F32)
        row = lax.broadcasted_iota(jnp.int32, out.shape, 0)
        mine = (row >= vlo_ref[v]) & (row < vhi_ref[v])
        y[...] = jnp.where(mine, out, 0.0 if first else y[...])

    @pl.when(v == 0)
    def _():
        gather(tokc_ref, x0, sem_in.at[0])
        wait_in(x0, sem_in.at[0])
        gather(tokn_ref, x1, sem_in.at[1])
        ffn(x0, y0, True)

    for par, (xc, xn, yc, yn) in enumerate(((x0, x1, y0, y1), (x1, x0, y1, y0))):
        here = live & (odd == (par == 1))

        @pl.when(here & new_block & (v > 0))
        def _():
            wait_in(xc, sem_in.at[par])

        @pl.when(here & new_block & (blk >= 2))
        def _():
            wait_out(yc, sem_out.at[par])

        @pl.when(here & new_block & (v > 0))
        def _():
            gather(tokn_ref, xn, sem_in.at[1 - par])
            scatter(yn, dstp_ref, sem_out.at[1 - par])
            ffn(xc, yc, True)

        @pl.when(here & jnp.logical_not(new_block))
        def _():
            ffn(xc, yc, False)

        @pl.when(here & (v == n_live - 1))
        def _():
            scatter(yc, dstc_ref, sem_out.at[par])
            wait_out(yc, sem_out.at[par])
            wait_in(xn, sem_in.at[1 - par])

        @pl.when(here & (v == n_live - 1) & (blk >= 1))
        def _():
            wait_out(yn, sem_out.at[1 - par])


def _ffn_call(vp, tok, dst, vis, w1, w3, w2, layer):
    n, dh = vp.shape
    d = 2 * dh
    de = w1.shape[-1]
    n_blocks = tok.shape[0]
    n_vis = vis[0].shape[0]
    nxt = lambda v, vb, ve, lo, hi, nv: (jnp.minimum(vb[v] + 1, n_blocks - 1), 0, 0)
    cur = lambda v, vb, ve, lo, hi, nv: (vb[v], 0, 0)
    prv = lambda v, vb, ve, lo, hi, nv: (jnp.maximum(vb[v] - 1, 0), 0, 0)
    idx_spec = lambda f: pl.BlockSpec((1, 1, MOE_TILE), f, memory_space=pltpu.SMEM)
    grid_spec = pltpu.PrefetchScalarGridSpec(
        num_scalar_prefetch=5,
        grid=(n_vis,),
        in_specs=[
            idx_spec(cur), idx_spec(nxt), idx_spec(prv), idx_spec(cur),
            pl.BlockSpec(memory_space=pl.ANY),
            pl.BlockSpec((1, 1, d, de), lambda v, vb, ve, lo, hi, nv: (layer, ve[v], 0, 0)),
            pl.BlockSpec((1, 1, d, de), lambda v, vb, ve, lo, hi, nv: (layer, ve[v], 0, 0)),
            pl.BlockSpec((1, 1, de, d), lambda v, vb, ve, lo, hi, nv: (layer, ve[v], 0, 0)),
        ],
        out_specs=pl.BlockSpec(memory_space=pl.ANY),
        scratch_shapes=[pltpu.VMEM((MOE_TILE, dh), jnp.uint32), pltpu.VMEM((MOE_TILE, dh), jnp.uint32),
                        pltpu.VMEM((MOE_TILE, d), F32), pltpu.VMEM((MOE_TILE, d), F32),
                        pltpu.VMEM((d, 2 * de), BF16), pltpu.VMEM((de, d), BF16),
                        pltpu.SemaphoreType.DMA((2,)), pltpu.SemaphoreType.DMA((2,))],
    )
    return pl.pallas_call(
        functools.partial(_ffn_kernel, n_blocks=n_blocks),
        grid_spec=grid_spec,
        out_shape=jax.ShapeDtypeStruct((n_blocks * MOE_TILE, d), F32),
        compiler_params=_cparams(("arbitrary",)),
        name="moe_ffn",
    )(*vis, tok, tok, dst, dst, vp, w1, w3, w2)


def _combine_kernel(yk_ref, h_ref, gate_ref, mod_ref, o_ref):
    d = h_ref.shape[1]
    gate = gate_ref[...]
    y = gate[:, 0:1] * yk_ref[:, :d] + gate[:, 1:2] * yk_ref[:, d:]
    o_ref[...] = h_ref[...] + _mod_row(mod_ref, 5) * y


def _combine_call(h, yk, gates, mod, tpb, nb):
    n, d = h.shape
    return pl.pallas_call(
        _combine_kernel,
        grid=(n // TM,),
        in_specs=[_row_spec(TOP_K * d, tpb), _row_spec(d, tpb), _row_spec(LANES, tpb), _mod_spec(d, tpb, nb)],
        out_specs=_row_spec(d, tpb),
        out_shape=jax.ShapeDtypeStruct((n, d), F32),
        compiler_params=_cparams(("parallel",)),
        name="moe_combine",
    )(yk, h, gates, mod)


def _routing_tables(ids, n_tok):
    n = n_tok * TOP_K
    expert = ids[:, :TOP_K].reshape(n)
    order = jnp.argsort(expert).astype(jnp.int32)
    n_blocks = n // MOE_TILE
    dst = order.reshape(n_blocks, 1, MOE_TILE)
    tok = dst // TOP_K
    e_iota = jnp.arange(N_EXPERTS, dtype=jnp.int32)
    counts = jnp.sum((expert[:, None] == e_iota[None, :]).astype(jnp.int32), axis=0)
    send = jnp.cumsum(counts)
    sstart = send - counts

    first_blk = sstart // MOE_TILE
    last_blk = jnp.maximum(send - 1, 0) // MOE_TILE
    nvis = jnp.where(counts > 0, last_blk - first_blk + 1, 0)
    vend = jnp.cumsum(nvis)
    vstart = vend - nvis
    n_vis_max = n_blocks + N_EXPERTS
    vi = jnp.arange(n_vis_max, dtype=jnp.int32)
    total = vend[-1]
    vc = jnp.minimum(vi, total - 1)
    ve = jnp.sum(vc[:, None] >= vend[None, :], axis=1).astype(jnp.int32)
    sel = ve[:, None] == e_iota[None, :]
    pick = lambda t: jnp.sum(jnp.where(sel, t[None, :], 0), axis=1)
    vb = pick(first_blk) + vc - pick(vstart)
    lo = jnp.maximum(pick(sstart), vb * MOE_TILE) - vb * MOE_TILE
    hi = jnp.minimum(pick(send), (vb + 1) * MOE_TILE) - vb * MOE_TILE
    vis = tuple(t.astype(jnp.int32) for t in (vb, ve, lo, hi, total.reshape(1)))
    return tok, dst, vis


def _moe_layer(h, g2n, mod, w_grp, b_grp, w_rt, b_rt, w1, w3, w2, layer, tpb, nb):
    n, d = h.shape
    pad = LANES - N_GROUPS - N_EXPERTS
    wr = jnp.concatenate([w_grp, w_rt, jnp.zeros((d, pad), F32)], axis=1)
    wr_hi = wr.astype(BF16)
    wr = jnp.concatenate([wr_hi, (wr - wr_hi.astype(F32)).astype(BF16)], axis=1)
    br = jnp.concatenate([b_grp, b_rt, jnp.zeros((pad,), F32)]).reshape(1, LANES)
    ids, gates, vp = _router_call(h, g2n, mod, wr, br, tpb, nb)
    tok, dst, vis = _routing_tables(ids, n)
    yk = _ffn_call(vp, tok, dst, vis, w1, w3, w2, layer)
    return _combine_call(h, yk.reshape(n, TOP_K * d), gates, mod, tpb, nb)


def _s5_layer(h, g1n, mod, prm, tpb, nb):
    n, d = h.shape
    nt = n // nb
    n_chunks = nt // S5_CHUNK
    xg = _s5_in_call(h, g1n, mod, tpb, nb)
    mt, pm, qm, a = _s5_weights(prm['lam_re'], prm['lam_im'], prm['log_dt'], prm['b_re'], prm['b_im'],
                                prm['c_re'], prm['c_im'])
    n_lat_chunks = (nt - prm['ctx_len']) // S5_CHUNK
    yg = _s5_core_call(xg, mt, pm, qm, a, n_chunks, n_lat_chunks, nb)
    return _glu_call(h, yg, g1n, mod, prm['d'], prm['w_glu'].astype(BF16), prm['b_glu'], tpb, nb)


def _rope_tables(seq, ctx_len):
    rows = seq // GRID_W
    t_row = jnp.repeat(jnp.arange(rows), GRID_W).astype(F32)
    t_col = jnp.tile(jnp.arange(GRID_W), rows).astype(F32)
    half = HEAD_DIM // 2
    inv = ROPE_BASE ** (-jnp.arange(0, half, 2, dtype=F32) / half)
    ar = t_row[:, None] * inv
    ac = t_col[:, None] * inv
    ang = jnp.concatenate([ar, ar, ac, ac], axis=-1)
    cos = jnp.cos(ang)
    sin = jnp.sin(ang)
    lane = jnp.arange(HEAD_DIM)
    sin = jnp.where((lane & 32) == 0, -sin, sin)
    cos = jnp.concatenate([cos, jnp.ones((ctx_len, HEAD_DIM), F32)], axis=0)
    sin = jnp.concatenate([sin, jnp.zeros((ctx_len, HEAD_DIM), F32)], axis=0)
    return cos, sin


def kernel(x, c, ctx, c_ctx, w_mod, b_mod, norm1_g, norm2_g, s5_lam_re, s5_lam_im, s5_log_dt, s5_b_re, s5_b_im, s5_c_re, s5_c_im, s5_d, s5_w_glu, s5_b_glu, swa_w_qkv, swa_q_g, swa_k_g, swa_sink, swa_w_o, ga_w_qkv, ga_q_g, ga_k_g, ga_w_o, moe_w_grp, moe_b_grp, moe_w_rt, moe_b_rt, moe_w1, moe_w3, moe_w2):
    nb, seq, d = x.shape
    ctx_len = ctx.shape[1]
    depth = w_mod.shape[0]
    nt = seq + ctx_len
    n = nb * nt
    tpb = nt // TM
    assert nt % TM == 0 and ctx_len == TM and nb < 8 and (TOP_K * n) % MOE_TILE == 0 and d % (S5_SUB * S5_GROUP) == 0

    c8 = jnp.zeros((8, d), F32).at[:nb].set(c).at[nb].set(c_ctx)
    mods = _ada_mod_all(c8, w_mod, b_mod).reshape(depth, 8, N_MOD, d)
    cos, sin = _rope_tables(seq, ctx_len)

    h = jnp.concatenate([x, ctx], axis=1).reshape(n, d)
    for i in range(depth):
        kind, j = i % 3, i // 3
        mod = mods[i]
        if kind == 0:
            prm = dict(lam_re=s5_lam_re[j], lam_im=s5_lam_im[j], log_dt=s5_log_dt[j], b_re=s5_b_re[j],
                       b_im=s5_b_im[j], c_re=s5_c_re[j], c_im=s5_c_im[j], d=s5_d[j], w_glu=s5_w_glu[j],
                       b_glu=s5_b_glu[j], ctx_len=ctx_len)
            h = _s5_layer(h, norm1_g[i], mod, prm, tpb, nb)
        elif kind == 1:
            q, k, v = _qkv_call(h, norm1_g[i], mod, cos, sin, swa_q_g[j], swa_k_g[j],
                                swa_w_qkv[j].astype(BF16), tpb, nb)
            o = _swa_call(q, k, v, swa_sink[j], nb, seq, ctx_len)
            h = _oproj_call(o, swa_w_o[j].astype(BF16), h, mod, tpb, nb)
        else:
            q, k, v = _qkv_call(h, norm1_g[i], mod, cos, sin, ga_q_g[j], ga_k_g[j],
                                ga_w_qkv[j].astype(BF16), tpb, nb)
            o = _ga_call(q, k, v, nb, seq, ctx_len)
            h = _oproj_call(o, ga_w_o[j].astype(BF16), h, mod, tpb, nb)
        h = _moe_layer(h, norm2_g[i], mod, moe_w_grp[i], moe_b_grp[i], moe_w_rt[i], moe_b_rt[i],
                       moe_w1, moe_w3, moe_w2, i, tpb, nb)
    return h.reshape(nb, nt, d)[:, :seq]
```
